```python
import jax, jax.numpy as jnp
from jax import lax
import numpy as np

D_MODEL = 2048
BATCH = 2
SEQ = 4096
DEPTH = 1

CHUNK = 64
CONV_W = 3
D_CONV = D_MODEL
N_HEADS = 16
N_KV_HEADS = 4
HEAD_DIM = 128
GROUP = N_HEADS // N_KV_HEADS
D_ATTN = N_HEADS * HEAD_DIM
D_KV = N_KV_HEADS * HEAD_DIM
IDX_HEADS = 16
IDX_DIM = 64
IDX_W_SCALE = (IDX_HEADS * IDX_DIM) ** -0.5
TOPK_MAX = 256
Q_BLOCK = 128
D_FF = 5632
ALPHA = (2.0 * DEPTH) ** 0.25
BETA = (8.0 * DEPTH) ** -0.25
LN_EPS = 1e-5
N_MOD = 6

_IN_SIZES = (D_CONV, D_CONV, D_CONV,
             D_ATTN, D_KV, D_KV,
             IDX_HEADS * IDX_DIM, IDX_DIM, IDX_HEADS,
             D_MODEL, D_MODEL)
D_IN = sum(_IN_SIZES)

kernel_name = "hybrid_shortconv_dsa_convffn_deepnorm_adaln"


def layer_norm(x, g, b):
    xf = x.astype(jnp.float32)
    mu = jnp.mean(xf, axis=-1, keepdims=True)
    var = jnp.mean(jnp.square(xf - mu), axis=-1, keepdims=True)
    y = (xf - mu) * lax.rsqrt(var + LN_EPS)
    return (y * g.astype(jnp.float32) + b.astype(jnp.float32)).astype(x.dtype)


def causal_dwconv(h, w):
    S = h.shape[1]
    hp = jnp.pad(h, ((0, 0), (CONV_W - 1, 0), (0, 0)))
    out = hp[:, 0:S] * w[0]
    for j in range(1, CONV_W):
        out = out + hp[:, j:j + S] * w[j]
    return out


def dsa_attention(q, k, v, q_idx, k_idx, w_idx):
    B, S = q.shape[0], q.shape[1]
    topk = min(TOPK_MAX, S // 4)
    nblk = S // Q_BLOCK
    key_pos = jnp.arange(S)
    f32 = jnp.float32
    k_idx32 = k_idx.astype(f32)

    def to_blocks(a):
        return jnp.swapaxes(a.reshape((B, nblk, Q_BLOCK) + a.shape[2:]), 0, 1)

    def block_fn(args):
        blk, qb, qib, wb = args
        q_pos = blk * Q_BLOCK + jnp.arange(Q_BLOCK)
        limit = (q_pos // CHUNK + 1) * CHUNK
        admissible = key_pos[None, :] < limit[:, None]
        dots = jnp.einsum('bqhd,bsd->bqhs', qib.astype(f32), k_idx32)
        score = jnp.einsum('bqh,bqhs->bqs', wb.astype(f32), jax.nn.relu(dots))
        score = jnp.where(admissible[None], score, -jnp.inf)
        _, sel = lax.top_k(score, topk)
        valid = sel < limit[None, :, None]
        ks = jax.vmap(lambda kb, ib: kb[ib])(k, sel)
        vs = jax.vmap(lambda vb, ib: vb[ib])(v, sel)
        qg = qb.reshape(B, Q_BLOCK, N_KV_HEADS, GROUP, HEAD_DIM)
        logits = jnp.einsum('bqngd,bqknd->bqngk', qg.astype(f32), ks.astype(f32)) * (HEAD_DIM ** -0.5)
        logits = jnp.where(valid[:, :, None, None, :], logits, -jnp.inf)
        p = jax.nn.softmax(logits, axis=-1)
        o = jnp.einsum('bqngk,bqknd->bqngd', p, vs.astype(f32))
        return o.reshape(B, Q_BLOCK, D_ATTN).astype(q.dtype)

    out = lax.map(block_fn, (jnp.arange(nblk), to_blocks(q), to_blocks(q_idx), to_blocks(w_idx)))
    return jnp.swapaxes(out, 0, 1).reshape(B, S, D_ATTN)


def setup_inputs(seed: int = 0) -> dict:
    key = jax.random.key(seed)
    ks = jax.random.split(key, 20)
    f32 = jnp.float32
    L = DEPTH

    def nrm(k, shape, scale):
        return jax.random.normal(k, shape, f32) * scale

    return {
        "x": nrm(ks[0], (BATCH, SEQ, D_MODEL), 1.0),
        "c": nrm(ks[1], (BATCH, D_MODEL), 1.0),
        "w_cond": nrm(ks[2], (L, D_MODEL, N_MOD * D_MODEL), 0.2 * D_MODEL ** -0.5),
        "b_cond": nrm(ks[3], (L, N_MOD * D_MODEL), 0.01),
        "w_in": nrm(ks[4], (L, D_MODEL, D_IN), D_MODEL ** -0.5),
        "conv_a": nrm(ks[5], (L, CONV_W, D_CONV), CONV_W ** -0.5),
        "idx_kn_g": 1.0 + nrm(ks[6], (L, IDX_DIM), 0.02),
        "idx_kn_b": nrm(ks[7], (L, IDX_DIM), 0.02),
        "w_a": nrm(ks[8], (L, D_CONV, D_MODEL), D_CONV ** -0.5),
        "w_b": nrm(ks[9], (L, D_ATTN, D_MODEL), D_ATTN ** -0.5),
        "w_o": nrm(ks[10], (L, D_MODEL, D_MODEL), BETA * D_MODEL ** -0.5),
        "ln1_g": 1.0 + nrm(ks[11], (L, D_MODEL), 0.02),
        "ln1_b": nrm(ks[12], (L, D_MODEL), 0.02),
        "w_up": nrm(ks[13], (L, D_MODEL, 2 * D_FF), D_MODEL ** -0.5),
        "conv_f": nrm(ks[14], (L, CONV_W, D_FF), CONV_W ** -0.5),
        "w_down": nrm(ks[15], (L, D_FF, D_MODEL), BETA * D_FF ** -0.5),
        "ln2_g": 1.0 + nrm(ks[16], (L, D_MODEL), 0.02),
        "ln2_b": nrm(ks[17], (L, D_MODEL), 0.02),
    }


def reference(x, c, w_cond, b_cond, w_in, conv_a, idx_kn_g, idx_kn_b, w_a, w_b, w_o,
              ln1_g, ln1_b, w_up, conv_f, w_down, ln2_g, ln2_b):
    B, S, D = x.shape
    offsets = [int(o) for o in np.cumsum(_IN_SIZES)[:-1]]
    c_act = jax.nn.silu(c)
    for l in range(DEPTH):
        mod = (jnp.einsum('bd,de->be', c_act, w_cond[l]) + b_cond[l])[:, None, :]
        sh_m, sc_m, g_m, sh_f, sc_f, g_f = jnp.split(mod, N_MOD, axis=-1)

        u = x * (1.0 + sc_m) + sh_m
        proj = jnp.einsum('bsd,de->bse', u, w_in[l])
        cb, cc, ch, q, k, v, qi, ki, wi, ga, gb = jnp.split(proj, offsets, axis=-1)

        y_a = cb * causal_dwconv(cc * ch, conv_a[l])

        ki = layer_norm(ki, idx_kn_g[l], idx_kn_b[l])
        y_b = dsa_attention(q.reshape(B, S, N_HEADS, HEAD_DIM),
                            k.reshape(B, S, N_KV_HEADS, HEAD_DIM),
                            v.reshape(B, S, N_KV_HEADS, HEAD_DIM),
                            qi.reshape(B, S, IDX_HEADS, IDX_DIM),
                            ki,
                            wi * IDX_W_SCALE)

        merged = (jax.nn.sigmoid(ga) * jnp.einsum('bsc,cd->bsd', y_a, w_a[l])
                  + jax.nn.sigmoid(gb) * jnp.einsum('bsc,cd->bsd', y_b, w_b[l]))
        mix_out = jnp.einsum('bsd,de->bse', merged, w_o[l])
        x = layer_norm(ALPHA * x + (1.0 + g_m) * mix_out, ln1_g[l], ln1_b[l])

        u = x * (1.0 + sc_f) + sh_f
        h_act, h_gate = jnp.split(jnp.einsum('bsd,df->bsf', u, w_up[l]), 2, axis=-1)
        h_act = causal_dwconv(h_act, conv_f[l])
        y = jnp.einsum('bsf,fd->bsd', jax.nn.gelu(h_act) * h_gate, w_down[l])
        x = layer_norm(ALPHA * x + (1.0 + g_f) * y, ln2_g[l], ln2_b[l])
    return x
```

```python
import functools

import jax
import jax.numpy as jnp
from jax import lax
from jax.experimental import pallas as pl
from jax.experimental.pallas import tpu as pltpu

F32 = jnp.float32
BF16 = jnp.bfloat16
I32 = jnp.int32

CHUNK = 64
N_HEADS = 16
N_KV_HEADS = 4
HEAD_DIM = 128
GROUP = N_HEADS // N_KV_HEADS
IDX_HEADS = 16
IDX_DIM = 64
IDX_W_SCALE = (IDX_HEADS * IDX_DIM) ** -0.5
TOPK_MAX = 256
LN_EPS = 1e-5
N_MOD = 6

LANES = 128
SUBLANES = 8
VMEM_LIMIT = 56 * 1024 * 1024

TM = 1024
TN_CONV = 512
TN_PROJ = 1024
TN_MERGE = 512
TM_OUT = 512
TM_FFN = 512
TF_FFN = 512
TQ = 128
TK = 256

NEG_BIAS = -1e30
M_INIT = -3e38
INT_MIN = -(2 ** 31)

_NT = (((1,), (1,)), ((), ()))


def _params(sem):
    return pltpu.CompilerParams(dimension_semantics=sem, vmem_limit_bytes=VMEM_LIMIT)


def _sigmoid(x):
    return 1.0 / (1.0 + jnp.exp(-x))


def _layer_norm_rows(r, g, b):
    mu = jnp.mean(r, axis=-1, keepdims=True)
    d = r - mu
    var = jnp.mean(d * d, axis=-1, keepdims=True)
    return d * lax.rsqrt(var + LN_EPS) * g + b


def _mod_kernel(c_ref, w_ref, b_ref, o_ref):
    c = c_ref[...]
    ca = (c * _sigmoid(c)).astype(BF16)
    o_ref[...] = jnp.dot(ca, w_ref[...].astype(BF16), preferred_element_type=F32) + b_ref[...]


def _mod_call(c_pad, w_cond, b_cond):
    rows, d = c_pad.shape
    n = w_cond.shape[1]
    tn = 1024
    return pl.pallas_call(
        _mod_kernel,
        out_shape=jax.ShapeDtypeStruct((rows, n), F32),
        grid=(n // tn,),
        in_specs=[pl.BlockSpec((rows, d), lambda j: (0, 0)),
                  pl.BlockSpec((d, tn), lambda j: (0, j)),
                  pl.BlockSpec((1, tn), lambda j: (0, j))],
        out_specs=pl.BlockSpec((rows, tn), lambda j: (0, j)),
        compiler_params=_params(("arbitrary",)),
        name="mod",
    )(c_pad, w_cond, b_cond)


def _causal_conv_rows(p, first, p_ref, halo_ref, j, cw_ref):
    tm = p.shape[0]
    @pl.when(first)
    def _():
        p_ref[0:SUBLANES, :] = jnp.zeros((SUBLANES, p.shape[1]), F32)

    @pl.when(jnp.logical_not(first))
    def _():
        p_ref[0:SUBLANES, :] = halo_ref[j]

    p_ref[SUBLANES:, :] = p
    halo_ref[j] = p[tm - SUBLANES:, :]
    return (cw_ref[0:1, :] * p_ref[SUBLANES - 2:SUBLANES - 2 + tm, :]
            + cw_ref[1:2, :] * p_ref[SUBLANES - 1:SUBLANES - 1 + tm, :]
            + cw_ref[2:3, :] * p)


def _conv_a_kernel(tiles_per_seq, x_ref, sc_ref, sh_ref, wb_ref, wc_ref, wh_ref, cw_ref,
                   o_ref, u_ref, p_ref, halo_ref):
    i = pl.program_id(0)
    j = pl.program_id(1)

    @pl.when(j == 0)
    def _():
        u_ref[...] = (x_ref[...] * (1.0 + sc_ref[...]) + sh_ref[...]).astype(BF16)

    u = u_ref[...]
    cb = jnp.dot(u, wb_ref[...], preferred_element_type=F32)
    cc = jnp.dot(u, wc_ref[...], preferred_element_type=F32)
    ch = jnp.dot(u, wh_ref[...], preferred_element_type=F32)
    first = (i % tiles_per_seq) == 0
    conv = _causal_conv_rows(cc * ch, first, p_ref, halo_ref, j, cw_ref)
    o_ref[...] = (cb * conv).astype(BF16)


def _conv_a_call(x2, mod4, w1, cw, seq):
    m, d = x2.shape
    dc = w1.shape[1] // 3
    nj = dc // TN_CONV
    tps = seq // TM
    row = lambda i, j: (i, 0)
    return pl.pallas_call(
        functools.partial(_conv_a_kernel, tps),
        out_shape=jax.ShapeDtypeStruct((m, dc), BF16),
        grid=(m // TM, nj),
        in_specs=[pl.BlockSpec((TM, d), row),
                  pl.BlockSpec((None, None, 1, d), lambda i, j: (i // tps, 1, 0, 0)),
                  pl.BlockSpec((None, None, 1, d), lambda i, j: (i // tps, 0, 0, 0)),
                  pl.BlockSpec((d, TN_CONV), lambda i, j: (0, j)),
                  pl.BlockSpec((d, TN_CONV), lambda i, j: (0, nj + j)),
                  pl.BlockSpec((d, TN_CONV), lambda i, j: (0, 2 * nj + j)),
                  pl.BlockSpec((SUBLANES, TN_CONV), lambda i, j: (0, j))],
        out_specs=pl.BlockSpec((TM, TN_CONV), lambda i, j: (i, j)),
        scratch_shapes=[pltpu.VMEM((TM, d), BF16),
                        pltpu.VMEM((TM + SUBLANES, TN_CONV), F32),
                        pltpu.VMEM((nj, SUBLANES, TN_CONV), F32)],
        compiler_params=_params(("arbitrary", "arbitrary")),
        name="conv_a",
    )(x2, mod4, mod4, w1, w1, w1, cw)


def _proj_kernel(x_ref, sc_ref, sh_ref, w_ref, cs_ref, wk_ref, kg_ref, kb_ref,
                 o_ref, ka_ref, kbo_ref, wi_ref, u_ref):
    j = pl.program_id(1)

    @pl.when(j == 0)
    def _():
        ub = (x_ref[...] * (1.0 + sc_ref[...]) + sh_ref[...]).astype(BF16)
        u_ref[...] = ub
        kw = jnp.dot(ub, wk_ref[...], preferred_element_type=F32)
        lane = lax.broadcasted_iota(I32, (1, LANES), 1)
        is_key = lane < IDX_DIM
        mu = jnp.sum(jnp.where(is_key, kw, 0.0), axis=-1, keepdims=True) * (1.0 / IDX_DIM)
        dk = jnp.where(is_key, kw - mu, 0.0)
        var = jnp.sum(dk * dk, axis=-1, keepdims=True) * (1.0 / IDX_DIM)
        kn = dk * lax.rsqrt(var + LN_EPS) * kg_ref[...] + kb_ref[...]
        ka_ref[...] = kn.astype(BF16)
        kbo_ref[...] = pltpu.roll(kn, IDX_DIM, axis=1).astype(BF16)
        wi_ref[...] = kw * IDX_W_SCALE

    acc = jnp.dot(u_ref[...], w_ref[...], preferred_element_type=F32)
    o_ref[...] = (acc * cs_ref[...]).astype(BF16)


def _proj_call(x2, mod4, w2, colscale, wk, kg, kb, seq):
    m, d = x2.shape
    n = w2.shape[1]
    tps = seq // TM
    small = lambda i, j: (i, 0)
    return pl.pallas_call(
        _proj_kernel,
        out_shape=(jax.ShapeDtypeStruct((m, n), BF16),
                   jax.ShapeDtypeStruct((m, LANES), BF16),
                   jax.ShapeDtypeStruct((m, LANES), BF16),
                   jax.ShapeDtypeStruct((m, LANES), F32)),
        grid=(m // TM, n // TN_PROJ),
        in_specs=[pl.BlockSpec((TM, d), small),
                  pl.BlockSpec((None, None, 1, d), lambda i, j: (i // tps, 1, 0, 0)),
                  pl.BlockSpec((None, None, 1, d), lambda i, j: (i // tps, 0, 0, 0)),
                  pl.BlockSpec((d, TN_PROJ), lambda i, j: (0, j)),
                  pl.BlockSpec((1, TN_PROJ), lambda i, j: (0, j)),
                  pl.BlockSpec((d, LANES), lambda i, j: (0, 0)),
                  pl.BlockSpec((1, LANES), lambda i, j: (0, 0)),
                  pl.BlockSpec((1, LANES), lambda i, j: (0, 0))],
        out_specs=(pl.BlockSpec((TM, TN_PROJ), lambda i, j: (i, j)),
                   pl.BlockSpec((TM, LANES), small),
                   pl.BlockSpec((TM, LANES), small),
                   pl.BlockSpec((TM, LANES), small)),
        scratch_shapes=[pltpu.VMEM((TM, d), BF16)],
        compiler_params=_params(("arbitrary", "arbitrary")),
        name="proj",
    )(x2, mod4, mod4, w2, colscale, wk, kg, kb)


def _attn_kernel(q_ref, k_ref, v_ref, qi_ref, ka_ref, kb_ref, wi_ref, o_ref, key_ref):
    i = pl.program_id(1)
    q0 = i * TQ
    n_kt = (q0 + TQ + TK - 1) // TK
    row = lax.broadcasted_iota(I32, (TQ, 1), 0)
    limit = ((q0 + row) // CHUNK + 1) * CHUNK
    wis = [wi_ref[:, IDX_DIM + h:IDX_DIM + h + 1] for h in range(IDX_HEADS)]
    lane_pos = lax.broadcasted_iota(I32, (TQ, TK), 1)

    def score_body(j, carry):
        ks = pl.multiple_of(j * TK, TK)
        ka = ka_ref[pl.ds(ks, TK), :]
        kb = kb_ref[pl.ds(ks, TK), :]
        acc = jnp.zeros((TQ, TK), F32)
        for p in range(IDX_HEADS // 2):
            qp = qi_ref[:, p * LANES:(p + 1) * LANES]
            d0 = lax.dot_general(qp, ka, _NT, preferred_element_type=F32)
            d1 = lax.dot_general(qp, kb, _NT, preferred_element_type=F32)
            acc = acc + wis[2 * p] * jnp.maximum(d0, 0.0) + wis[2 * p + 1] * jnp.maximum(d1, 0.0)
        bits = pltpu.bitcast(acc, I32)
        key = bits ^ ((bits >> 31) & 0x7FFFFFFF)
        key_ref[j] = jnp.where(ks + lane_pos < limit, key, INT_MIN)
        return carry

    lax.fori_loop(0, n_kt, score_body, 0)

    def count_ge(cand):
        cb = jnp.broadcast_to(cand, (TQ, LANES))

        def body(j, c):
            kt = key_ref[j]
            for h in range(TK // LANES):
                c = c + jnp.where(kt[:, h * LANES:(h + 1) * LANES] >= cb, 1.0, 0.0)
            return c

        c = lax.fori_loop(0, n_kt, body, jnp.zeros((TQ, LANES), F32))
        return jnp.sum(c, axis=1, keepdims=True)

    kf = float(TOPK_MAX)
    t0 = jnp.where(count_ge(jnp.zeros((TQ, 1), I32)) >= kf, 0, INT_MIN).astype(I32)

    def bit_body(b, t):
        cand = t | lax.shift_left(jnp.int32(1), 30 - b)
        return jnp.where(count_ge(cand) >= kf, cand, t)

    thr = lax.fori_loop(0, 31, bit_body, t0)
    thr = jnp.maximum(thr, INT_MIN + 1)
    thr_b = jnp.broadcast_to(thr, (TQ, TK))

    def bias_body(j, carry):
        bias = jnp.where(key_ref[j] >= thr_b, 0.0, NEG_BIAS).astype(F32)
        key_ref[j] = pltpu.bitcast(bias, I32)
        return carry

    lax.fori_loop(0, n_kt, bias_body, 0)

    rows = GROUP * TQ
    for n in range(N_KV_HEADS):
        qs = jnp.concatenate(
            [q_ref[:, (n * GROUP + g) * HEAD_DIM:(n * GROUP + g + 1) * HEAD_DIM] for g in range(GROUP)],
            axis=0)

        def kv_body(j, carry, n=n, qs=qs):
            m_i, l_i, acc = carry
            ks = pl.multiple_of(j * TK, TK)
            kt = k_ref[pl.ds(ks, TK), n * HEAD_DIM:(n + 1) * HEAD_DIM]
            vt = v_ref[pl.ds(ks, TK), n * HEAD_DIM:(n + 1) * HEAD_DIM]
            s = lax.dot_general(qs, kt, _NT, preferred_element_type=F32)
            bias = pltpu.bitcast(key_ref[j], F32)
            s = (s.reshape(GROUP, TQ, TK) + bias[None]).reshape(rows, TK)
            m_new = jnp.maximum(m_i, jnp.max(s, axis=-1, keepdims=True))
            alpha = jnp.exp(m_i - m_new)
            p = jnp.exp(s - m_new)
            l_new = alpha * l_i + jnp.sum(p, axis=-1, keepdims=True)
            acc_new = alpha * acc + jnp.dot(p.astype(BF16), vt, preferred_element_type=F32)
            return m_new, l_new, acc_new

        init = (jnp.full((rows, 1), M_INIT, F32), jnp.zeros((rows, 1), F32),
                jnp.zeros((rows, HEAD_DIM), F32))
        _, l_f, acc_f = lax.fori_loop(0, n_kt, kv_body, init)
        o = acc_f / l_f
        for g in range(GROUP):
            h = n * GROUP + g
            o_ref[:, h * HEAD_DIM:(h + 1) * HEAD_DIM] = o[g * TQ:(g + 1) * TQ].astype(BF16)


def _attn_call(p2, ka, kb, wi, batch, seq):
    d_attn = N_HEADS * HEAD_DIM
    d_kv = N_KV_HEADS * HEAD_DIM
    d_qi = IDX_HEADS * IDX_DIM
    nq = seq // TQ
    k_blk = d_attn // d_kv
    qi_blk = (d_attn + 2 * d_kv) // d_qi
    return pl.pallas_call(
        _attn_kernel,
        out_shape=jax.ShapeDtypeStruct((batch * seq, d_attn), BF16),
        grid=(batch, nq),
        in_specs=[pl.BlockSpec((TQ, d_attn), lambda b, i: (b * nq + i, 0)),
                  pl.BlockSpec((seq, d_kv), lambda b, i: (b, k_blk)),
                  pl.BlockSpec((seq, d_kv), lambda b, i: (b, k_blk + 1)),
                  pl.BlockSpec((TQ, d_qi), lambda b, i: (b * nq + i, qi_blk)),
                  pl.BlockSpec((seq, LANES), lambda b, i: (b, 0)),
                  pl.BlockSpec((seq, LANES), lambda b, i: (b, 0)),
                  pl.BlockSpec((TQ, LANES), lambda b, i: (b * nq + i, 0))],
        out_specs=pl.BlockSpec((TQ, d_attn), lambda b, i: (b * nq + i, 0)),
        scratch_shapes=[pltpu.VMEM((seq // TK, TQ, TK), I32)],
        compiler_params=_params(("arbitrary", "arbitrary")),
        name="attn",
    )(p2, p2, p2, p2, ka, kb, wi)


def _merge_kernel(ya_ref, yb_ref, wa_ref, wb_ref, ga_ref, gb_ref, o_ref):
    a = jnp.dot(ya_ref[...], wa_ref[...], preferred_element_type=F32)
    b = jnp.dot(yb_ref[...], wb_ref[...], preferred_element_type=F32)
    o_ref[...] = (_sigmoid(ga_ref[...].astype(F32)) * a
                  + _sigmoid(gb_ref[...].astype(F32)) * b).astype(BF16)


def _merge_call(ya, yb, wa, wb, p2, ga_col):
    m, d = ya.shape
    n = wa.shape[1]
    ga_blk = ga_col // TN_MERGE
    gb_blk = (ga_col + n) // TN_MERGE
    return pl.pallas_call(
        _merge_kernel,
        out_shape=jax.ShapeDtypeStruct((m, n), BF16),
        grid=(m // TM, n // TN_MERGE),
        in_specs=[pl.BlockSpec((TM, d), lambda i, j: (i, 0)),
                  pl.BlockSpec((TM, d), lambda i, j: (i, 0)),
                  pl.BlockSpec((d, TN_MERGE), lambda i, j: (0, j)),
                  pl.BlockSpec((d, TN_MERGE), lambda i, j: (0, j)),
                  pl.BlockSpec((TM, TN_MERGE), lambda i, j: (i, ga_blk + j)),
                  pl.BlockSpec((TM, TN_MERGE), lambda i, j: (i, gb_blk + j))],
        out_specs=pl.BlockSpec((TM, TN_MERGE), lambda i, j: (i, j)),
        compiler_params=_params(("arbitrary", "arbitrary")),
        name="merge",
    )(ya, yb, wa, wb, p2, p2)


def _out_kernel(alpha, mg_ref, wo_ref, x_ref, gate_ref, g_ref, b_ref, o_ref):
    mix = jnp.dot(mg_ref[...], wo_ref[...], preferred_element_type=F32)
    r = alpha * x_ref[...] + (1.0 + gate_ref[...]) * mix
    o_ref[...] = _layer_norm_rows(r, g_ref[...], b_ref[...])


def _out_call(merged, wo, x2, mod4, g, b, seq, alpha):
    m, d = x2.shape
    tps = seq // TM_OUT
    row = lambda i: (i, 0)
    full = lambda i: (0, 0)
    return pl.pallas_call(
        functools.partial(_out_kernel, alpha),
        out_shape=jax.ShapeDtypeStruct((m, d), F32),
        grid=(m // TM_OUT,),
        in_specs=[pl.BlockSpec((TM_OUT, merged.shape[1]), row),
                  pl.BlockSpec(wo.shape, full),
                  pl.BlockSpec((TM_OUT, d), row),
                  pl.BlockSpec((None, None, 1, d), lambda i: (i // tps, 2, 0, 0)),
                  pl.BlockSpec((1, d), full),
                  pl.BlockSpec((1, d), full)],
        out_specs=pl.BlockSpec((TM_OUT, d), row),
        compiler_params=_params(("arbitrary",)),
        name="out",
    )(merged, wo, x2, mod4, g, b)


def _gelu_tanh(x):
    return 0.5 * x * (1.0 + jnp.tanh(0.7978845608028654 * (x + 0.044715 * (x * x * x))))


def _ffn_kernel(alpha, tiles_per_seq, nf, x_ref, sc_ref, sh_ref, gate_ref, wa_ref, wb_ref, cw_ref,
                wd_ref, g_ref, b_ref, o_ref, u_ref, acc_ref, p_ref, halo_ref):
    i = pl.program_id(0)
    j = pl.program_id(1)

    @pl.when(j == 0)
    def _():
        u_ref[...] = (x_ref[...] * (1.0 + sc_ref[...]) + sh_ref[...]).astype(BF16)

    u = u_ref[...]
    ha = jnp.dot(u, wa_ref[...], preferred_element_type=F32)
    hb = jnp.dot(u, wb_ref[...], preferred_element_type=F32)
    first = (i % tiles_per_seq) == 0
    conv = _causal_conv_rows(ha, first, p_ref, halo_ref, j, cw_ref)
    act = (_gelu_tanh(conv) * hb).astype(BF16)
    contrib = jnp.dot(act, wd_ref[...], preferred_element_type=F32)

    @pl.when(j == 0)
    def _():
        acc_ref[...] = contrib

    @pl.when(j > 0)
    def _():
        acc_ref[...] += contrib

    @pl.when(j == nf - 1)
    def _():
        r = alpha * x_ref[...] + (1.0 + gate_ref[...]) * acc_ref[...]
        o_ref[...] = _layer_norm_rows(r, g_ref[...], b_ref[...])


def _ffn_call(x1, mod4, wup, cw, wdown, g, b, seq, alpha):
    m, d = x1.shape
    dff = wdown.shape[0]
    nf = dff // TF_FFN
    tps = seq // TM_FFN
    row = lambda i, j: (i, 0)
    full = lambda i, j: (0, 0)
    return pl.pallas_call(
        functools.partial(_ffn_kernel, alpha, tps, nf),
        out_shape=jax.ShapeDtypeStruct((m, d), F32),
        grid=(m // TM_FFN, nf),
        in_specs=[pl.BlockSpec((TM_FFN, d), row),
                  pl.BlockSpec((None, None, 1, d), lambda i, j: (i // tps, 4, 0, 0)),
                  pl.BlockSpec((None, None, 1, d), lambda i, j: (i // tps, 3, 0, 0)),
                  pl.BlockSpec((None, None, 1, d), lambda i, j: (i // tps, 5, 0, 0)),
                  pl.BlockSpec((d, TF_FFN), lambda i, j: (0, j)),
                  pl.BlockSpec((d, TF_FFN), lambda i, j: (0, nf + j)),
                  pl.BlockSpec((SUBLANES, TF_FFN), lambda i, j: (0, j)),
                  pl.BlockSpec((TF_FFN, d), lambda i, j: (j, 0)),
                  pl.BlockSpec((1, d), full),
                  pl.BlockSpec((1, d), full)],
        out_specs=pl.BlockSpec((TM_FFN, d), row),
        scratch_shapes=[pltpu.VMEM((TM_FFN, d), BF16),
                        pltpu.VMEM((TM_FFN, d), F32),
                        pltpu.VMEM((TM_FFN + SUBLANES, TF_FFN), F32),
                        pltpu.VMEM((nf, SUBLANES, TF_FFN), F32)],
        compiler_params=_params(("arbitrary", "arbitrary")),
        name="ffn",
    )(x1, mod4, mod4, mod4, wup, wup, cw, wdown, g, b)


def _pad_rows(a, rows):
    return jnp.pad(a, ((0, rows - a.shape[0]), (0, 0)))


def kernel(x, c, w_cond, b_cond, w_in, conv_a, idx_kn_g, idx_kn_b, w_a, w_b, w_o, ln1_g, ln1_b,
           w_up, conv_f, w_down, ln2_g, ln2_b):
    batch, seq, d = x.shape
    depth = w_cond.shape[0]
    alpha = (2.0 * depth) ** 0.25
    d_attn = N_HEADS * HEAD_DIM
    d_kv = N_KV_HEADS * HEAD_DIM
    d_qi = IDX_HEADS * IDX_DIM
    d_conv = conv_a.shape[2]
    assert min(TOPK_MAX, seq // 4) == TOPK_MAX and seq % TM == 0 and seq % TK == 0
    o_q = 3 * d_conv
    o_ki = o_q + d_attn + 2 * d_kv + d_qi
    o_g = o_ki + IDX_DIM + IDX_HEADS
    n_main = d_attn + 2 * d_kv + d_qi
    colscale = jnp.concatenate([jnp.full((1, d_attn), HEAD_DIM ** -0.5, F32),
                                jnp.ones((1, n_main - d_attn + 2 * d), F32)], axis=1)

    x2 = x.reshape(batch * seq, d)
    c_pad = _pad_rows(c, 2 * SUBLANES)
    for l in range(depth):
        mod = _mod_call(c_pad, w_cond[l], b_cond[l][None, :])[:batch]
        mod4 = mod.reshape(batch, N_MOD, 1, d)

        w1 = w_in[l][:, :o_q].astype(BF16)
        w2 = jnp.concatenate([w_in[l][:, o_q:o_ki], w_in[l][:, o_g:]], axis=1).astype(BF16)
        wk = jnp.pad(w_in[l][:, o_ki:o_g], ((0, 0), (0, LANES - (o_g - o_ki)))).astype(BF16)
        kg = jnp.pad(idx_kn_g[l], (0, LANES - IDX_DIM))[None, :]
        kb = jnp.pad(idx_kn_b[l], (0, LANES - IDX_DIM))[None, :]

        y_a = _conv_a_call(x2, mod4, w1, _pad_rows(conv_a[l], SUBLANES), seq)
        p2, ka, kbo, wi = _proj_call(x2, mod4, w2, colscale, wk, kg, kb, seq)
        y_b = _attn_call(p2, ka, kbo, wi, batch, seq)
        merged = _merge_call(y_a, y_b, w_a[l].astype(BF16), w_b[l].astype(BF16), p2, n_main)
        x2 = _out_call(merged, w_o[l].astype(BF16), x2, mod4, ln1_g[l][None, :], ln1_b[l][None, :],
                       seq, alpha)
        x2 = _ffn_call(x2, mod4, w_up[l].astype(BF16), _pad_rows(conv_f[l], SUBLANES),
                       w_down[l].astype(BF16), ln2_g[l][None, :], ln2_b[l][None, :], seq, alpha)
    return x2.reshape(batch, seq, d)
```

```python
import functools
import math

import jax
import jax.numpy as jnp
from jax import lax
from jax.experimental import pallas as pl
from jax.experimental.pallas import tpu as pltpu

F32 = jnp.float32
BF16 = jnp.bfloat16
I32 = jnp.int32

CHUNK = 64
N_HEADS = 16
N_KV_HEADS = 4
HEAD_DIM = 128
GROUP = N_HEADS // N_KV_HEADS
IDX_HEADS = 16
IDX_DIM = 64
IDX_W_SCALE = (IDX_HEADS * IDX_DIM) ** -0.5
TOPK_MAX = 256
LN_EPS = 1e-5
N_MOD = 6

LANES = 128
SUBLANES = 8
VMEM_LIMIT = 56 * 1024 * 1024

TM = 1024
TN_CONV = 512
TN_PROJ = 1536
TN_MERGE = 512
TM_OUT = 512
TM_FFN = 512
TF_FFN = 512
TQ = 512
TK = 256
ONES_ROWS = 16
V_ROWS = HEAD_DIM + ONES_ROWS

NEG_BIAS = -1e30
M_INIT = -3e38
INT_MIN = -(2 ** 31)

_NT = (((1,), (1,)), ((), ()))


def _params(sem):
    return pltpu.CompilerParams(dimension_semantics=sem, vmem_limit_bytes=VMEM_LIMIT)


def _sigmoid(x):
    return 1.0 / (1.0 + jnp.exp(-x))


def _layer_norm_rows(r, g, b):
    mu = jnp.mean(r, axis=-1, keepdims=True)
    d = r - mu
    var = jnp.mean(d * d, axis=-1, keepdims=True)
    return d * lax.rsqrt(var + LN_EPS) * g + b


def _mod_kernel(c_ref, w_ref, b_ref, o_ref):
    c = c_ref[...]
    ca = (c * _sigmoid(c)).astype(BF16)
    o_ref[...] = jnp.dot(ca, w_ref[...].astype(BF16), preferred_element_type=F32) + b_ref[...]


def _mod_call(c_pad, w_cond, b_cond):
    rows, d = c_pad.shape
    n = w_cond.shape[1]
    tn = 1024
    return pl.pallas_call(
        _mod_kernel,
        out_shape=jax.ShapeDtypeStruct((rows, n), F32),
        grid=(n // tn,),
        in_specs=[pl.BlockSpec((rows, d), lambda j: (0, 0)),
                  pl.BlockSpec((d, tn), lambda j: (0, j)),
                  pl.BlockSpec((1, tn), lambda j: (0, j))],
        out_specs=pl.BlockSpec((rows, tn), lambda j: (0, j)),
        compiler_params=_params(("arbitrary",)),
        name="mod",
    )(c_pad, w_cond, b_cond)


def _causal_conv_rows(p, first, p_ref, halo_ref, j, cw_ref):
    tm = p.shape[0]

    @pl.when(first)
    def _():
        p_ref[0:SUBLANES, :] = jnp.zeros((SUBLANES, p.shape[1]), F32)

    @pl.when(jnp.logical_not(first))
    def _():
        p_ref[0:SUBLANES, :] = halo_ref[j]

    p_ref[SUBLANES:, :] = p
    halo_ref[j] = p[tm - SUBLANES:, :]
    return (cw_ref[0:1, :] * p_ref[SUBLANES - 2:SUBLANES - 2 + tm, :]
            + cw_ref[1:2, :] * p_ref[SUBLANES - 1:SUBLANES - 1 + tm, :]
            + cw_ref[2:3, :] * p)


def _conv_a_kernel(tiles_per_seq, x_ref, sc_ref, sh_ref, wb_ref, wc_ref, wh_ref, cw_ref,
                   o_ref, u_ref, p_ref, halo_ref):
    i = pl.program_id(0)
    j = pl.program_id(1)

    @pl.when(j == 0)
    def _():
        u_ref[...] = (x_ref[...] * (1.0 + sc_ref[...]) + sh_ref[...]).astype(BF16)

    u = u_ref[...]
    cb = jnp.dot(u, wb_ref[...], preferred_element_type=F32)
    cc = jnp.dot(u, wc_ref[...], preferred_element_type=F32)
    ch = jnp.dot(u, wh_ref[...], preferred_element_type=F32)
    first = (i % tiles_per_seq) == 0
    conv = _causal_conv_rows(cc * ch, first, p_ref, halo_ref, j, cw_ref)
    o_ref[...] = (cb * conv).astype(BF16)


def _conv_a_call(x2, mod4, w1, cw, seq):
    m, d = x2.shape
    dc = w1.shape[1] // 3
    nj = dc // TN_CONV
    tps = seq // TM
    row = lambda i, j: (i, 0)
    return pl.pallas_call(
        functools.partial(_conv_a_kernel, tps),
        out_shape=jax.ShapeDtypeStruct((m, dc), BF16),
        grid=(m // TM, nj),
        in_specs=[pl.BlockSpec((TM, d), row),
                  pl.BlockSpec((None, None, 1, d), lambda i, j: (i // tps, 1, 0, 0)),
                  pl.BlockSpec((None, None, 1, d), lambda i, j: (i // tps, 0, 0, 0)),
                  pl.BlockSpec((d, TN_CONV), lambda i, j: (0, j)),
                  pl.BlockSpec((d, TN_CONV), lambda i, j: (0, nj + j)),
                  pl.BlockSpec((d, TN_CONV), lambda i, j: (0, 2 * nj + j)),
                  pl.BlockSpec((SUBLANES, TN_CONV), lambda i, j: (0, j))],
        out_specs=pl.BlockSpec((TM, TN_CONV), lambda i, j: (i, j)),
        scratch_shapes=[pltpu.VMEM((TM, d), BF16),
                        pltpu.VMEM((TM + SUBLANES, TN_CONV), F32),
                        pltpu.VMEM((nj, SUBLANES, TN_CONV), F32)],
        compiler_params=_params(("arbitrary", "arbitrary")),
        name="conv_a",
    )(x2, mod4, mod4, w1, w1, w1, cw)


def _proj_kernel(x_ref, sc_ref, sh_ref, w_ref, cs_ref, wk_ref, kg_ref, kb_ref, wvt_ref,
                 o_ref, ka_ref, kbo_ref, wit_ref, vt_ref, u_ref):
    j = pl.program_id(1)

    @pl.when(j == 0)
    def _():
        ub = (x_ref[...] * (1.0 + sc_ref[...]) + sh_ref[...]).astype(BF16)
        u_ref[...] = ub
        kw = jnp.dot(ub, wk_ref[...], preferred_element_type=F32)
        lane = lax.broadcasted_iota(I32, (1, LANES), 1)
        is_key = lane < IDX_DIM
        mu = jnp.sum(jnp.where(is_key, kw, 0.0), axis=-1, keepdims=True) * (1.0 / IDX_DIM)
        dk = jnp.where(is_key, kw - mu, 0.0)
        var = jnp.sum(dk * dk, axis=-1, keepdims=True) * (1.0 / IDX_DIM)
        kn = dk * lax.rsqrt(var + LN_EPS) * kg_ref[...] + kb_ref[...]
        ka_ref[...] = kn.astype(BF16)
        kbo_ref[...] = pltpu.roll(kn, IDX_DIM, axis=1).astype(BF16)
        wit_ref[...] = (kw * IDX_W_SCALE).T
        vt = lax.dot_general(wvt_ref[...], ub, _NT, preferred_element_type=F32).astype(BF16)
        ones = jnp.ones((ONES_ROWS, TK), BF16)
        for t in range(vt_ref.shape[0]):
            for n in range(N_KV_HEADS):
                vt_ref[t, n * V_ROWS:n * V_ROWS + HEAD_DIM, :] = (
                    vt[n * HEAD_DIM:(n + 1) * HEAD_DIM, t * TK:(t + 1) * TK])
                vt_ref[t, n * V_ROWS + HEAD_DIM:(n + 1) * V_ROWS, :] = ones

    acc = jnp.dot(u_ref[...], w_ref[...], preferred_element_type=F32)
    o_ref[...] = (acc * cs_ref[...]).astype(BF16)


def _proj_call(x2, mod4, w2, colscale, wk, kg, kb, wvt, seq):
    m, d = x2.shape
    n = w2.shape[1]
    d_kv = wvt.shape[0]
    tps = seq // TM
    small = lambda i, j: (i, 0)
    full = lambda i, j: (0, 0)
    return pl.pallas_call(
        _proj_kernel,
        out_shape=(jax.ShapeDtypeStruct((m, n), BF16),
                   jax.ShapeDtypeStruct((m, LANES), BF16),
                   jax.ShapeDtypeStruct((m, LANES), BF16),
                   jax.ShapeDtypeStruct((LANES, m), F32),
                   jax.ShapeDtypeStruct((m // TK, N_KV_HEADS * V_ROWS, TK), BF16)),
        grid=(m // TM, n // TN_PROJ),
        in_specs=[pl.BlockSpec((TM, d), small),
                  pl.BlockSpec((None, None, 1, d), lambda i, j: (i // tps, 1, 0, 0)),
                  pl.BlockSpec((None, None, 1, d), lambda i, j: (i // tps, 0, 0, 0)),
                  pl.BlockSpec((d, TN_PROJ), lambda i, j: (0, j)),
                  pl.BlockSpec((1, TN_PROJ), lambda i, j: (0, j)),
                  pl.BlockSpec((d, LANES), full),
                  pl.BlockSpec((1, LANES), full),
                  pl.BlockSpec((1, LANES), full),
                  pl.BlockSpec((d_kv, d), full)],
        out_specs=(pl.BlockSpec((TM, TN_PROJ), lambda i, j: (i, j)),
                   pl.BlockSpec((TM, LANES), small),
                   pl.BlockSpec((TM, LANES), small),
                   pl.BlockSpec((LANES, TM), lambda i, j: (0, i)),
                   pl.BlockSpec((TM // TK, N_KV_HEADS * V_ROWS, TK), lambda i, j: (i, 0, 0))),
        scratch_shapes=[pltpu.VMEM((TM, d), BF16)],
        compiler_params=_params(("arbitrary", "arbitrary")),
        name="proj",
    )(x2, mod4, mod4, w2, colscale, wk, kg, kb, wvt)


RED_ROWS = 32
CNT_ROWS = 16


def _attn_kernel(q_ref, k_ref, vt_ref, qi_ref, ka_ref, kb_ref, wit_ref, o_ref,
                 key_ref, sa_ref, sb_ref, p_ref, m_ref, acc_ref):
    i = pl.program_id(1)
    q0 = i * TQ
    n_kt = (q0 + TQ) // TK
    q_pos = q0 + lax.broadcasted_iota(I32, (1, TQ), 1)
    limit = (q_pos // CHUNK + 1) * CHUNK
    key_iota = lax.broadcasted_iota(I32, (TK, TQ), 0)

    def score_body(j, carry):
        ks = pl.multiple_of(j * TK, TK)
        ka = ka_ref[pl.ds(ks, TK), :]
        kb = kb_ref[pl.ds(ks, TK), :]
        acc = None
        for p in range(IDX_HEADS // 2):
            qp = qi_ref[:, p * LANES:(p + 1) * LANES]
            d0 = lax.dot_general(ka, qp, _NT, preferred_element_type=F32)
            d1 = lax.dot_general(kb, qp, _NT, preferred_element_type=F32)
            w0 = wit_ref[IDX_DIM + 2 * p:IDX_DIM + 2 * p + 1, :]
            w1 = wit_ref[IDX_DIM + 2 * p + 1:IDX_DIM + 2 * p + 2, :]
            t = w0 * jnp.maximum(d0, 0.0) + w1 * jnp.maximum(d1, 0.0)
            acc = t if acc is None else acc + t
        bits = pltpu.bitcast(acc, I32)
        key = bits ^ ((bits >> 31) & 0x7FFFFFFF)
        key_ref[j] = jnp.where(ks + key_iota < limit, key, INT_MIN)
        return carry

    lax.fori_loop(0, n_kt, score_body, 0)

    def count_ge(cand):
        def body(j, c):
            hit = jnp.where(key_ref[j] >= cand, 1.0, 0.0)
            return c + jnp.sum(hit.reshape(TK // CNT_ROWS, CNT_ROWS, TQ), axis=0)

        c = lax.fori_loop(0, n_kt, body, jnp.zeros((CNT_ROWS, TQ), F32))
        return jnp.sum(c, axis=0, keepdims=True)

    kf = float(TOPK_MAX)
    t0 = jnp.where(count_ge(jnp.zeros((1, TQ), I32)) >= kf, 0, INT_MIN).astype(I32)

    def bit_body(b, t):
        cand = t | lax.shift_left(jnp.int32(1), 30 - b)
        return jnp.where(count_ge(cand) >= kf, cand, t)

    thr = lax.fori_loop(0, 31, bit_body, t0)
    thr = jnp.maximum(thr, INT_MIN + 1)

    def bias_body(j, carry):
        bias = jnp.where(key_ref[j] >= thr, 0.0, NEG_BIAS).astype(F32)
        key_ref[j] = pltpu.bitcast(bias, I32)
        return carry

    lax.fori_loop(0, n_kt, bias_body, 0)

    n_ch = TK // RED_ROWS
    for n in range(N_KV_HEADS):
        qg = [q_ref[:, (n * GROUP + g) * HEAD_DIM:(n * GROUP + g + 1) * HEAD_DIM] for g in range(GROUP)]
        m_ref[...] = jnp.full(m_ref.shape, M_INIT, F32)
        acc_ref[...] = jnp.zeros(acc_ref.shape, F32)

        def logits(j, dst_ref, n=n, qg=qg):
            ks = pl.multiple_of(j * TK, TK)
            kt = k_ref[pl.ds(ks, TK), n * HEAD_DIM:(n + 1) * HEAD_DIM]
            bias = pltpu.bitcast(key_ref[j], F32)
            for g in range(GROUP):
                dst_ref[g] = lax.dot_general(kt, qg[g], _NT, preferred_element_type=F32) + bias

        def softmax_pv(j, src_ref, slot, n=n):
            vt = vt_ref[j, n * V_ROWS:(n + 1) * V_ROWS, :]
            for g in range(GROUP):
                mx = src_ref[g, 0:RED_ROWS, :]
                for c in range(1, n_ch):
                    mx = jnp.maximum(mx, src_ref[g, c * RED_ROWS:(c + 1) * RED_ROWS, :])
                m_old = m_ref[g]
                m_new = jnp.maximum(m_old, jnp.max(mx, axis=0, keepdims=True))
                alpha = jnp.exp2(m_old - m_new)
                m_ref[g] = m_new
                for c in range(n_ch):
                    rows = slice(c * RED_ROWS, (c + 1) * RED_ROWS)
                    p_ref[slot, g, rows, :] = jnp.exp2(src_ref[g, rows, :] - m_new).astype(BF16)
                acc_ref[g] = alpha * acc_ref[g] + jnp.dot(vt, p_ref[slot, g], preferred_element_type=F32)

        logits(0, sa_ref)

        def pair_body(jj, carry):
            j = 2 * jj
            logits(j + 1, sb_ref)
            softmax_pv(j, sa_ref, 0)
            logits(jnp.minimum(j + 2, n_kt - 1), sa_ref)
            softmax_pv(j + 1, sb_ref, 1)
            return carry

        lax.fori_loop(0, n_kt // 2, pair_body, 0)
        for g in range(GROUP):
            h = n * GROUP + g
            o = acc_ref[g, 0:HEAD_DIM, :] / acc_ref[g, HEAD_DIM:HEAD_DIM + 1, :]
            o_ref[:, h * HEAD_DIM:(h + 1) * HEAD_DIM] = o.T.astype(BF16)


def _attn_call(p2, ka, kb, wit, vt, batch, seq, cols):
    d_attn = N_HEADS * HEAD_DIM
    d_kv = N_KV_HEADS * HEAD_DIM
    d_qi = IDX_HEADS * IDX_DIM
    nq = seq // TQ
    nkt = seq // TK
    once = pl.Buffered(1)
    return pl.pallas_call(
        _attn_kernel,
        out_shape=jax.ShapeDtypeStruct((batch * seq, d_attn), BF16),
        grid=(batch, nq),
        in_specs=[pl.BlockSpec((TQ, d_attn), lambda b, i: (b * nq + i, cols["q"] // d_attn)),
                  pl.BlockSpec((seq, d_kv), lambda b, i: (b, cols["k"] // d_kv), pipeline_mode=once),
                  pl.BlockSpec((nkt, N_KV_HEADS * V_ROWS, TK), lambda b, i: (b, 0, 0), pipeline_mode=once),
                  pl.BlockSpec((TQ, d_qi), lambda b, i: (b * nq + i, cols["qi"] // d_qi)),
                  pl.BlockSpec((seq, LANES), lambda b, i: (b, 0), pipeline_mode=once),
                  pl.BlockSpec((seq, LANES), lambda b, i: (b, 0), pipeline_mode=once),
                  pl.BlockSpec((LANES, TQ), lambda b, i: (0, b * nq + i))],
        out_specs=pl.BlockSpec((TQ, d_attn), lambda b, i: (b * nq + i, 0)),
        scratch_shapes=[pltpu.VMEM((nkt, TK, TQ), I32),
                        pltpu.VMEM((GROUP, TK, TQ), F32),
                        pltpu.VMEM((GROUP, TK, TQ), F32),
                        pltpu.VMEM((2, GROUP, TK, TQ), BF16),
                        pltpu.VMEM((GROUP, 1, TQ), F32),
                        pltpu.VMEM((GROUP, V_ROWS, TQ), F32)],
        compiler_params=_params(("arbitrary", "arbitrary")),
        name="attn",
    )(p2, p2, vt, p2, ka, kb, wit)


def _merge_kernel(ya_ref, yb_ref, wa_ref, wb_ref, ga_ref, gb_ref, o_ref):
    a = jnp.dot(ya_ref[...], wa_ref[...], preferred_element_type=F32)
    b = jnp.dot(yb_ref[...], wb_ref[...], preferred_element_type=F32)
    o_ref[...] = (_sigmoid(ga_ref[...].astype(F32)) * a
                  + _sigmoid(gb_ref[...].astype(F32)) * b).astype(BF16)


def _merge_call(ya, yb, wa, wb, p2, cols):
    m, d = ya.shape
    n = wa.shape[1]
    ga_blk = cols["ga"] // TN_MERGE
    gb_blk = cols["gb"] // TN_MERGE
    return pl.pallas_call(
        _merge_kernel,
        out_shape=jax.ShapeDtypeStruct((m, n), BF16),
        grid=(m // TM, n // TN_MERGE),
        in_specs=[pl.BlockSpec((TM, d), lambda i, j: (i, 0)),
                  pl.BlockSpec((TM, d), lambda i, j: (i, 0)),
                  pl.BlockSpec((d, TN_MERGE), lambda i, j: (0, j)),
                  pl.BlockSpec((d, TN_MERGE), lambda i, j: (0, j)),
                  pl.BlockSpec((TM, TN_MERGE), lambda i, j: (i, ga_blk + j)),
                  pl.BlockSpec((TM, TN_MERGE), lambda i, j: (i, gb_blk + j))],
        out_specs=pl.BlockSpec((TM, TN_MERGE), lambda i, j: (i, j)),
        compiler_params=_params(("arbitrary", "arbitrary")),
        name="merge",
    )(ya, yb, wa, wb, p2, p2)


def _out_kernel(alpha, mg_ref, wo_ref, x_ref, gate_ref, g_ref, b_ref, o_ref):
    mix = jnp.dot(mg_ref[...], wo_ref[...], preferred_element_type=F32)
    r = alpha * x_ref[...] + (1.0 + gate_ref[...]) * mix
    o_ref[...] = _layer_norm_rows(r, g_ref[...], b_ref[...])


def _out_call(merged, wo, x2, mod4, g, b, seq, alpha):
    m, d = x2.shape
    tps = seq // TM_OUT
    row = lambda i: (i, 0)
    full = lambda i: (0, 0)
    return pl.pallas_call(
        functools.partial(_out_kernel, alpha),
        out_shape=jax.ShapeDtypeStruct((m, d), F32),
        grid=(m // TM_OUT,),
        in_specs=[pl.BlockSpec((TM_OUT, merged.shape[1]), row),
                  pl.BlockSpec(wo.shape, full),
                  pl.BlockSpec((TM_OUT, d), row),
                  pl.BlockSpec((None, None, 1, d), lambda i: (i // tps, 2, 0, 0)),
                  pl.BlockSpec((1, d), full),
                  pl.BlockSpec((1, d), full)],
        out_specs=pl.BlockSpec((TM_OUT, d), row),
        compiler_params=_params(("arbitrary",)),
        name="out",
    )(merged, wo, x2, mod4, g, b)


def _gelu_tanh(x):
    return 0.5 * x * (1.0 + jnp.tanh(0.7978845608028654 * (x + 0.044715 * (x * x * x))))


def _ffn_kernel(alpha, tiles_per_seq, nf, x_ref, sc_ref, sh_ref, gate_ref, wa_ref, wb_ref, cw_ref,
                wd_ref, g_ref, b_ref, o_ref, u_ref, acc_ref, p_ref, halo_ref):
    i = pl.program_id(0)
    j = pl.program_id(1)

    @pl.when(j == 0)
    def _():
        u_ref[...] = (x_ref[...] * (1.0 + sc_ref[...]) + sh_ref[...]).astype(BF16)

    u = u_ref[...]
    ha = jnp.dot(u, wa_ref[...], preferred_element_type=F32)
    hb = jnp.dot(u, wb_ref[...], preferred_element_type=F32)
    first = (i % tiles_per_seq) == 0
    conv = _causal_conv_rows(ha, first, p_ref, halo_ref, j, cw_ref)
    act = (_gelu_tanh(conv) * hb).astype(BF16)
    contrib = jnp.dot(act, wd_ref[...], preferred_element_type=F32)

    @pl.when(j == 0)
    def _():
        acc_ref[...] = contrib

    @pl.when(j > 0)
    def _():
        acc_ref[...] += contrib

    @pl.when(j == nf - 1)
    def _():
        r = alpha * x_ref[...] + (1.0 + gate_ref[...]) * acc_ref[...]
        o_ref[...] = _layer_norm_rows(r, g_ref[...], b_ref[...])


def _ffn_call(x1, mod4, wup, cw, wdown, g, b, seq, alpha):
    m, d = x1.shape
    dff = wdown.shape[0]
    nf = dff // TF_FFN
    tps = seq // TM_FFN
    row = lambda i, j: (i, 0)
    full = lambda i, j: (0, 0)
    return pl.pallas_call(
        functools.partial(_ffn_kernel, alpha, tps, nf),
        out_shape=jax.ShapeDtypeStruct((m, d), F32),
        grid=(m // TM_FFN, nf),
        in_specs=[pl.BlockSpec((TM_FFN, d), row),
                  pl.BlockSpec((None, None, 1, d), lambda i, j: (i // tps, 4, 0, 0)),
                  pl.BlockSpec((None, None, 1, d), lambda i, j: (i // tps, 3, 0, 0)),
                  pl.BlockSpec((None, None, 1, d), lambda i, j: (i // tps, 5, 0, 0)),
                  pl.BlockSpec((d, TF_FFN), lambda i, j: (0, j)),
                  pl.BlockSpec((d, TF_FFN), lambda i, j: (0, nf + j)),
                  pl.BlockSpec((SUBLANES, TF_FFN), lambda i, j: (0, j)),
                  pl.BlockSpec((TF_FFN, d), lambda i, j: (j, 0)),
                  pl.BlockSpec((1, d), full),
                  pl.BlockSpec((1, d), full)],
        out_specs=pl.BlockSpec((TM_FFN, d), row),
        scratch_shapes=[pltpu.VMEM((TM_FFN, d), BF16),
                        pltpu.VMEM((TM_FFN, d), F32),
                        pltpu.VMEM((TM_FFN + SUBLANES, TF_FFN), F32),
                        pltpu.VMEM((nf, SUBLANES, TF_FFN), F32)],
        compiler_params=_params(("arbitrary", "arbitrary")),
        name="ffn",
    )(x1, mod4, mod4, mod4, wup, wup, cw, wdown, g, b)


def _pad_rows(a, rows):
    return jnp.pad(a, ((0, rows - a.shape[0]), (0, 0)))


def kernel(x, c, w_cond, b_cond, w_in, conv_a, idx_kn_g, idx_kn_b, w_a, w_b, w_o, ln1_g, ln1_b,
           w_up, conv_f, w_down, ln2_g, ln2_b):
    batch, seq, d = x.shape
    depth = w_cond.shape[0]
    alpha = (2.0 * depth) ** 0.25
    d_attn = N_HEADS * HEAD_DIM
    d_kv = N_KV_HEADS * HEAD_DIM
    d_qi = IDX_HEADS * IDX_DIM
    d_conv = conv_a.shape[2]
    assert min(TOPK_MAX, seq // 4) == TOPK_MAX and seq % TM == 0 and TM % TK == 0 and TQ % (2 * TK) == 0
    o_q = 3 * d_conv
    o_k = o_q + d_attn
    o_v = o_k + d_kv
    o_qi = o_v + d_kv
    o_ki = o_qi + d_qi
    o_g = o_ki + IDX_DIM + IDX_HEADS
    cols = {"q": 0, "qi": d_attn, "k": d_attn + d_qi, "ga": d_attn + d_qi + d_kv,
            "gb": d_attn + d_qi + d_kv + d}
    q_scale = HEAD_DIM ** -0.5 * math.log2(math.e)
    colscale = jnp.concatenate([jnp.full((1, d_attn), q_scale, F32),
                                jnp.ones((1, d_qi + d_kv + 2 * d), F32)], axis=1)

    x2 = x.reshape(batch * seq, d)
    c_pad = _pad_rows(c, 2 * SUBLANES)
    for l in range(depth):
        mod = _mod_call(c_pad, w_cond[l], b_cond[l][None, :])[:batch]
        mod4 = mod.reshape(batch, N_MOD, 1, d)

        wl = w_in[l]
        w1 = wl[:, :o_q].astype(BF16)
        w2 = jnp.concatenate([wl[:, o_q:o_k], wl[:, o_qi:o_ki], wl[:, o_k:o_v], wl[:, o_g:]],
                             axis=1).astype(BF16)
        wvt = wl[:, o_v:o_qi].T.astype(BF16)
        wk = jnp.pad(wl[:, o_ki:o_g], ((0, 0), (0, LANES - (o_g - o_ki)))).astype(BF16)
        kg = jnp.pad(idx_kn_g[l], (0, LANES - IDX_DIM))[None, :]
        kb = jnp.pad(idx_kn_b[l], (0, LANES - IDX_DIM))[None, :]

        y_a = _conv_a_call(x2, mod4, w1, _pad_rows(conv_a[l], SUBLANES), seq)
        p2, ka, kbo, wit, vt = _proj_call(x2, mod4, w2, colscale, wk, kg, kb, wvt, seq)
        y_b = _attn_call(p2, ka, kbo, wit, vt, batch, seq, cols)
        merged = _merge_call(y_a, y_b, w_a[l].astype(BF16), w_b[l].astype(BF16), p2, cols)
        x2 = _out_call(merged, w_o[l].astype(BF16), x2, mod4, ln1_g[l][None, :], ln1_b[l][None, :],
                       seq, alpha)
        x2 = _ffn_call(x2, mod4, w_up[l].astype(BF16), _pad_rows(conv_f[l], SUBLANES),
                       w_down[l].astype(BF16), ln2_g[l][None, :], ln2_b[l][None, :], seq, alpha)
    return x2.reshape(batch, seq, d)
```

```python
import functools
import math

import jax
import jax.numpy as jnp
from jax import lax
from jax.experimental import pallas as pl
from jax.experimental.pallas import tpu as pltpu

F32 = jnp.float32
BF16 = jnp.bfloat16
I32 = jnp.int32

CHUNK = 64
N_HEADS = 16
N_KV_HEADS = 4
HEAD_DIM = 128
GROUP = N_HEADS // N_KV_HEADS
IDX_HEADS = 16
IDX_DIM = 64
IDX_W_SCALE = (IDX_HEADS * IDX_DIM) ** -0.5
TOPK_MAX = 256
LN_EPS = 1e-5
N_MOD = 6

LANES = 128
SUBLANES = 8
VMEM_LIMIT = 56 * 1024 * 1024

TM = 1024
TN_CONV = 512
CONV_HALF = 256
TN_PROJ = 1536
TN_MERGE = 512
TM_OUT = 512
TM_FFN = 512
TF_FFN = 512
FFN_HALF = 256
TQ = 512
TK = 256
ONES_ROWS = 16
V_ROWS = HEAD_DIM + ONES_ROWS

NEG_BIAS = -1e30
M_INIT = -3e38
INT_MIN = -(2 ** 31)

_NT = (((1,), (1,)), ((), ()))


def _params(sem):
    return pltpu.CompilerParams(dimension_semantics=sem, vmem_limit_bytes=VMEM_LIMIT)


def _sigmoid(x):
    return 1.0 / (1.0 + jnp.exp(-x))


def _layer_norm_rows(r, g, b):
    mu = jnp.mean(r, axis=-1, keepdims=True)
    d = r - mu
    var = jnp.mean(d * d, axis=-1, keepdims=True)
    return d * lax.rsqrt(var + LN_EPS) * g + b


def _mod_kernel(c_ref, w_ref, b_ref, o_ref):
    c = c_ref[...]
    ca = (c * _sigmoid(c)).astype(BF16)
    o_ref[...] = jnp.dot(ca, w_ref[...].astype(BF16), preferred_element_type=F32) + b_ref[...]


def _mod_call(c_pad, w_cond, b_cond):
    rows, d = c_pad.shape
    n = w_cond.shape[1]
    tn = 1024
    return pl.pallas_call(
        _mod_kernel,
        out_shape=jax.ShapeDtypeStruct((rows, n), F32),
        grid=(n // tn,),
        in_specs=[pl.BlockSpec((rows, d), lambda j: (0, 0)),
                  pl.BlockSpec((d, tn), lambda j: (0, j)),
                  pl.BlockSpec((1, tn), lambda j: (0, j))],
        out_specs=pl.BlockSpec((rows, tn), lambda j: (0, j)),
        compiler_params=_params(("arbitrary",)),
        name="mod",
    )(c_pad, w_cond, b_cond)


def _causal_conv_rows(p, first, p_ref, halo_ref, j, cw_ref):
    tm = p.shape[0]
    p_ref[0:SUBLANES, :] = jnp.where(first, 0.0, halo_ref[j])
    p_ref[SUBLANES:, :] = p
    halo_ref[j] = p[tm - SUBLANES:, :]
    return (cw_ref[0:1, :] * p_ref[SUBLANES - 2:SUBLANES - 2 + tm, :]
            + cw_ref[1:2, :] * p_ref[SUBLANES - 1:SUBLANES - 1 + tm, :]
            + cw_ref[2:3, :] * p)


def _conv_a_kernel(tiles_per_seq, x_ref, sc_ref, sh_ref, wb_ref, wc_ref, wh_ref, cw_ref,
                   o_ref, u_ref, p_ref, halo_ref):
    i = pl.program_id(0)
    j = pl.program_id(1)

    @pl.when(j == 0)
    def _():
        u_ref[...] = (x_ref[...] * (1.0 + sc_ref[...]) + sh_ref[...]).astype(BF16)

    @pl.when((i == 0) & (j == 0))
    def _():
        halo_ref[...] = jnp.zeros(halo_ref.shape, F32)

    u = u_ref[...]
    first = (i % tiles_per_seq) == 0
    halves = [slice(h * CONV_HALF, (h + 1) * CONV_HALF) for h in range(TN_CONV // CONV_HALF)]
    proj = [tuple(jnp.dot(u, w_ref[:, cols], preferred_element_type=F32) for w_ref in (wb_ref, wc_ref, wh_ref))
            for cols in halves]
    for cols, (cb, cc, ch) in zip(halves, proj):
        conv = _causal_conv_rows(cc * ch, first, p_ref.at[:, cols], halo_ref.at[:, :, cols], j,
                                 cw_ref.at[:, cols])
        o_ref[:, cols] = (cb * conv).astype(BF16)


def _conv_a_call(x2, mod4, w1, cw, seq):
    m, d = x2.shape
    dc = w1.shape[1] // 3
    nj = dc // TN_CONV
    tps = seq // TM
    row = lambda i, j: (i, 0)
    return pl.pallas_call(
        functools.partial(_conv_a_kernel, tps),
        out_shape=jax.ShapeDtypeStruct((m, dc), BF16),
        grid=(m // TM, nj),
        in_specs=[pl.BlockSpec((TM, d), row),
                  pl.BlockSpec((None, None, 1, d), lambda i, j: (i // tps, 1, 0, 0)),
                  pl.BlockSpec((None, None, 1, d), lambda i, j: (i // tps, 0, 0, 0)),
                  pl.BlockSpec((d, TN_CONV), lambda i, j: (0, j)),
                  pl.BlockSpec((d, TN_CONV), lambda i, j: (0, nj + j)),
                  pl.BlockSpec((d, TN_CONV), lambda i, j: (0, 2 * nj + j)),
                  pl.BlockSpec((SUBLANES, TN_CONV), lambda i, j: (0, j))],
        out_specs=pl.BlockSpec((TM, TN_CONV), lambda i, j: (i, j)),
        scratch_shapes=[pltpu.VMEM((TM, d), BF16),
                        pltpu.VMEM((TM + SUBLANES, TN_CONV), F32),
                        pltpu.VMEM((nj, SUBLANES, TN_CONV), F32)],
        compiler_params=_params(("arbitrary", "arbitrary")),
        name="conv_a",
    )(x2, mod4, mod4, w1, w1, w1, cw)


def _proj_kernel(x_ref, sc_ref, sh_ref, w_ref, cs_ref, wk_ref, kg_ref, kb_ref, wvt_ref,
                 o_ref, ka_ref, kbo_ref, wit_ref, vt_ref, u_ref):
    j = pl.program_id(1)

    @pl.when(j == 0)
    def _():
        ub = (x_ref[...] * (1.0 + sc_ref[...]) + sh_ref[...]).astype(BF16)
        u_ref[...] = ub
        kw = jnp.dot(ub, wk_ref[...], preferred_element_type=F32)
        lane = lax.broadcasted_iota(I32, (1, LANES), 1)
        is_key = lane < IDX_DIM
        mu = jnp.sum(jnp.where(is_key, kw, 0.0), axis=-1, keepdims=True) * (1.0 / IDX_DIM)
        dk = jnp.where(is_key, kw - mu, 0.0)
        var = jnp.sum(dk * dk, axis=-1, keepdims=True) * (1.0 / IDX_DIM)
        kn = dk * lax.rsqrt(var + LN_EPS) * kg_ref[...] + kb_ref[...]
        ka_ref[...] = kn.astype(BF16)
        kbo_ref[...] = pltpu.roll(kn, IDX_DIM, axis=1).astype(BF16)
        wit_ref[...] = (kw * IDX_W_SCALE).T
        vt = lax.dot_general(wvt_ref[...], ub, _NT, preferred_element_type=F32).astype(BF16)
        ones = jnp.ones((ONES_ROWS, TK), BF16)
        for t in range(vt_ref.shape[0]):
            for n in range(N_KV_HEADS):
                vt_ref[t, n * V_ROWS:n * V_ROWS + HEAD_DIM, :] = (
                    vt[n * HEAD_DIM:(n + 1) * HEAD_DIM, t * TK:(t + 1) * TK])
                vt_ref[t, n * V_ROWS + HEAD_DIM:(n + 1) * V_ROWS, :] = ones

    acc = jnp.dot(u_ref[...], w_ref[...], preferred_element_type=F32)
    o_ref[...] = (acc * cs_ref[...]).astype(BF16)


def _proj_call(x2, mod4, w2, colscale, wk, kg, kb, wvt, seq):
    m, d = x2.shape
    n = w2.shape[1]
    d_kv = wvt.shape[0]
    tps = seq // TM
    small = lambda i, j: (i, 0)
    full = lambda i, j: (0, 0)
    return pl.pallas_call(
        _proj_kernel,
        out_shape=(jax.ShapeDtypeStruct((m, n), BF16),
                   jax.ShapeDtypeStruct((m, LANES), BF16),
                   jax.ShapeDtypeStruct((m, LANES), BF16),
                   jax.ShapeDtypeStruct((LANES, m), F32),
                   jax.ShapeDtypeStruct((m // TK, N_KV_HEADS * V_ROWS, TK), BF16)),
        grid=(m // TM, n // TN_PROJ),
        in_specs=[pl.BlockSpec((TM, d), small),
                  pl.BlockSpec((None, None, 1, d), lambda i, j: (i // tps, 1, 0, 0)),
                  pl.BlockSpec((None, None, 1, d), lambda i, j: (i // tps, 0, 0, 0)),
                  pl.BlockSpec((d, TN_PROJ), lambda i, j: (0, j)),
                  pl.BlockSpec((1, TN_PROJ), lambda i, j: (0, j)),
                  pl.BlockSpec((d, LANES), full),
                  pl.BlockSpec((1, LANES), full),
                  pl.BlockSpec((1, LANES), full),
                  pl.BlockSpec((d_kv, d), full)],
        out_specs=(pl.BlockSpec((TM, TN_PROJ), lambda i, j: (i, j)),
                   pl.BlockSpec((TM, LANES), small),
                   pl.BlockSpec((TM, LANES), small),
                   pl.BlockSpec((LANES, TM), lambda i, j: (0, i)),
                   pl.BlockSpec((TM // TK, N_KV_HEADS * V_ROWS, TK), lambda i, j: (i, 0, 0))),
        scratch_shapes=[pltpu.VMEM((TM, d), BF16)],
        compiler_params=_params(("arbitrary", "arbitrary")),
        name="proj",
    )(x2, mod4, mod4, w2, colscale, wk, kg, kb, wvt)


RED_ROWS = 32
CNT_ROWS = 16


def _attn_kernel(q_ref, k_ref, vt_ref, qi_ref, ka_ref, kb_ref, wit_ref, o_ref,
                 key_ref, sa_ref, sb_ref, ma_ref, mb_ref, p_ref, m_ref, acc_ref):
    i = pl.program_id(1)
    q0 = i * TQ
    n_kt = (q0 + TQ) // TK
    q_pos = q0 + lax.broadcasted_iota(I32, (1, TQ), 1)
    limit = (q_pos // CHUNK + 1) * CHUNK
    key_iota = lax.broadcasted_iota(I32, (TK, TQ), 0)

    def score_body(j, carry):
        ks = pl.multiple_of(j * TK, TK)
        ka = ka_ref[pl.ds(ks, TK), :]
        kb = kb_ref[pl.ds(ks, TK), :]
        acc = None
        for p in range(IDX_HEADS // 2):
            qp = qi_ref[:, p * LANES:(p + 1) * LANES]
            d0 = lax.dot_general(ka, qp, _NT, preferred_element_type=F32)
            d1 = lax.dot_general(kb, qp, _NT, preferred_element_type=F32)
            w0 = wit_ref[IDX_DIM + 2 * p:IDX_DIM + 2 * p + 1, :]
            w1 = wit_ref[IDX_DIM + 2 * p + 1:IDX_DIM + 2 * p + 2, :]
            t = w0 * jnp.maximum(d0, 0.0) + w1 * jnp.maximum(d1, 0.0)
            acc = t if acc is None else acc + t
        bits = pltpu.bitcast(acc, I32)
        key = bits ^ ((bits >> 31) & 0x7FFFFFFF)
        key_ref[j] = jnp.where(ks + key_iota < limit, key, INT_MIN)
        return carry

    lax.fori_loop(0, n_kt, score_body, 0)

    def count_ge(cand):
        def body(j, c):
            for r in range(TK // CNT_ROWS):
                c = c + jnp.where(key_ref[j, r * CNT_ROWS:(r + 1) * CNT_ROWS, :] >= cand, 1.0, 0.0)
            return c

        c = lax.fori_loop(0, n_kt, body, jnp.zeros((CNT_ROWS, TQ), F32))
        return jnp.sum(c, axis=0, keepdims=True)

    kf = float(TOPK_MAX)
    t0 = jnp.where(count_ge(jnp.zeros((1, TQ), I32)) >= kf, 0, INT_MIN).astype(I32)

    def bit_body(b, t):
        cand = t | lax.shift_left(jnp.int32(1), 30 - b)
        return jnp.where(count_ge(cand) >= kf, cand, t)

    thr = lax.fori_loop(0, 31, bit_body, t0)
    thr = jnp.maximum(thr, INT_MIN + 1)

    def bias_body(j, carry):
        bias = jnp.where(key_ref[j] >= thr, 0.0, NEG_BIAS).astype(F32)
        key_ref[j] = pltpu.bitcast(bias, I32)
        return carry

    lax.fori_loop(0, n_kt, bias_body, 0)

    n_ch = TK // RED_ROWS
    for n in range(N_KV_HEADS):
        qg = [q_ref[:, (n * GROUP + g) * HEAD_DIM:(n * GROUP + g + 1) * HEAD_DIM] for g in range(GROUP)]
        m_ref[...] = jnp.full(m_ref.shape, M_INIT, F32)
        acc_ref[...] = jnp.zeros(acc_ref.shape, F32)

        def logits(j, dst_ref, max_ref, n=n, qg=qg):
            ks = pl.multiple_of(j * TK, TK)
            kt = k_ref[pl.ds(ks, TK), n * HEAD_DIM:(n + 1) * HEAD_DIM]
            bias = pltpu.bitcast(key_ref[j], F32)
            for g in range(GROUP):
                s = lax.dot_general(kt, qg[g], _NT, preferred_element_type=F32) + bias
                dst_ref[g] = s
                part = jnp.max(s.reshape(n_ch, RED_ROWS, TQ), axis=0)
                max_ref[g] = jnp.max(part, axis=0, keepdims=True)

        def softmax_pv(j, src_ref, max_ref, slot, n=n):
            vt = vt_ref[j, n * V_ROWS:(n + 1) * V_ROWS, :]
            for g in range(GROUP):
                m_old = m_ref[g]
                m_new = jnp.maximum(m_old, max_ref[g])
                alpha = jnp.exp2(m_old - m_new)
                m_ref[g] = m_new
                for c in range(n_ch):
                    rows = slice(c * RED_ROWS, (c + 1) * RED_ROWS)
                    p_ref[slot, g, rows, :] = jnp.exp2((src_ref[g, rows, :] - m_new).astype(BF16))
                acc_ref[g] = alpha * acc_ref[g] + jnp.dot(vt, p_ref[slot, g], preferred_element_type=F32)

        def pair_body(jj, carry):
            j = 2 * jj
            logits(j, sa_ref, ma_ref)
            logits(j + 1, sb_ref, mb_ref)
            softmax_pv(j, sa_ref, ma_ref, 0)
            softmax_pv(j + 1, sb_ref, mb_ref, 1)
            return carry

        lax.fori_loop(0, n_kt // 2, pair_body, 0)
        for g in range(GROUP):
            h = n * GROUP + g
            o = acc_ref[g, 0:HEAD_DIM, :] / acc_ref[g, HEAD_DIM:HEAD_DIM + 1, :]
            o_ref[:, h * HEAD_DIM:(h + 1) * HEAD_DIM] = o.T.astype(BF16)


def _attn_call(p2, ka, kb, wit, vt, batch, seq, cols):
    d_attn = N_HEADS * HEAD_DIM
    d_kv = N_KV_HEADS * HEAD_DIM
    d_qi = IDX_HEADS * IDX_DIM
    nq = seq // TQ
    nkt = seq // TK
    once = pl.Buffered(1)
    return pl.pallas_call(
        _attn_kernel,
        out_shape=jax.ShapeDtypeStruct((batch * seq, d_attn), BF16),
        grid=(batch, nq),
        in_specs=[pl.BlockSpec((TQ, d_attn), lambda b, i: (b * nq + i, cols["q"] // d_attn)),
                  pl.BlockSpec((seq, d_kv), lambda b, i: (b, cols["k"] // d_kv), pipeline_mode=once),
                  pl.BlockSpec((nkt, N_KV_HEADS * V_ROWS, TK), lambda b, i: (b, 0, 0), pipeline_mode=once),
                  pl.BlockSpec((TQ, d_qi), lambda b, i: (b * nq + i, cols["qi"] // d_qi)),
                  pl.BlockSpec((seq, LANES), lambda b, i: (b, 0), pipeline_mode=once),
                  pl.BlockSpec((seq, LANES), lambda b, i: (b, 0), pipeline_mode=once),
                  pl.BlockSpec((LANES, TQ), lambda b, i: (0, b * nq + i))],
        out_specs=pl.BlockSpec((TQ, d_attn), lambda b, i: (b * nq + i, 0)),
        scratch_shapes=[pltpu.VMEM((nkt, TK, TQ), I32),
                        pltpu.VMEM((GROUP, TK, TQ), F32),
                        pltpu.VMEM((GROUP, TK, TQ), F32),
                        pltpu.VMEM((GROUP, 1, TQ), F32),
                        pltpu.VMEM((GROUP, 1, TQ), F32),
                        pltpu.VMEM((2, GROUP, TK, TQ), BF16),
                        pltpu.VMEM((GROUP, 1, TQ), F32),
                        pltpu.VMEM((GROUP, V_ROWS, TQ), F32)],
        compiler_params=_params(("arbitrary", "arbitrary")),
        name="attn",
    )(p2, p2, vt, p2, ka, kb, wit)


def _merge_kernel(ya_ref, yb_ref, wa_ref, wb_ref, ga_ref, gb_ref, o_ref):
    a = jnp.dot(ya_ref[...], wa_ref[...], preferred_element_type=F32)
    b = jnp.dot(yb_ref[...], wb_ref[...], preferred_element_type=F32)
    o_ref[...] = (_sigmoid(ga_ref[...].astype(F32)) * a
                  + _sigmoid(gb_ref[...].astype(F32)) * b).astype(BF16)


def _merge_call(ya, yb, wa, wb, p2, cols):
    m, d = ya.shape
    n = wa.shape[1]
    ga_blk = cols["ga"] // TN_MERGE
    gb_blk = cols["gb"] // TN_MERGE
    return pl.pallas_call(
        _merge_kernel,
        out_shape=jax.ShapeDtypeStruct((m, n), BF16),
        grid=(m // TM, n // TN_MERGE),
        in_specs=[pl.BlockSpec((TM, d), lambda i, j: (i, 0)),
                  pl.BlockSpec((TM, d), lambda i, j: (i, 0)),
                  pl.BlockSpec((d, TN_MERGE), lambda i, j: (0, j)),
                  pl.BlockSpec((d, TN_MERGE), lambda i, j: (0, j)),
                  pl.BlockSpec((TM, TN_MERGE), lambda i, j: (i, ga_blk + j)),
                  pl.BlockSpec((TM, TN_MERGE), lambda i, j: (i, gb_blk + j))],
        out_specs=pl.BlockSpec((TM, TN_MERGE), lambda i, j: (i, j)),
        compiler_params=_params(("arbitrary", "arbitrary")),
        name="merge",
    )(ya, yb, wa, wb, p2, p2)


def _out_kernel(alpha, mg_ref, wo_ref, x_ref, gate_ref, g_ref, b_ref, o_ref):
    mix = jnp.dot(mg_ref[...], wo_ref[...], preferred_element_type=F32)
    r = alpha * x_ref[...] + (1.0 + gate_ref[...]) * mix
    o_ref[...] = _layer_norm_rows(r, g_ref[...], b_ref[...])


def _out_call(merged, wo, x2, mod4, g, b, seq, alpha):
    m, d = x2.shape
    tps = seq // TM_OUT
    row = lambda i: (i, 0)
    full = lambda i: (0, 0)
    return pl.pallas_call(
        functools.partial(_out_kernel, alpha),
        out_shape=jax.ShapeDtypeStruct((m, d), F32),
        grid=(m // TM_OUT,),
        in_specs=[pl.BlockSpec((TM_OUT, merged.shape[1]), row),
                  pl.BlockSpec(wo.shape, full),
                  pl.BlockSpec((TM_OUT, d), row),
                  pl.BlockSpec((None, None, 1, d), lambda i: (i // tps, 2, 0, 0)),
                  pl.BlockSpec((1, d), full),
                  pl.BlockSpec((1, d), full)],
        out_specs=pl.BlockSpec((TM_OUT, d), row),
        compiler_params=_params(("arbitrary",)),
        name="out",
    )(merged, wo, x2, mod4, g, b)


def _gelu_tanh(x):
    return 0.5 * x * (1.0 + jnp.tanh(0.7978845608028654 * (x + 0.044715 * (x * x * x))))


def _ffn_kernel(alpha, tiles_per_seq, nf, x_ref, sc_ref, sh_ref, gate_ref, wa_ref, wb_ref, cw_ref,
                wd_ref, g_ref, b_ref, o_ref, u_ref, acc_ref, p_ref, halo_ref):
    i = pl.program_id(0)
    j = pl.program_id(1)

    @pl.when(j == 0)
    def _():
        u_ref[...] = (x_ref[...] * (1.0 + sc_ref[...]) + sh_ref[...]).astype(BF16)
        acc_ref[...] = jnp.zeros(acc_ref.shape, F32)

    @pl.when((i == 0) & (j == 0))
    def _():
        halo_ref[...] = jnp.zeros(halo_ref.shape, F32)

    u = u_ref[...]
    first = (i % tiles_per_seq) == 0
    halves = [slice(h * FFN_HALF, (h + 1) * FFN_HALF) for h in range(TF_FFN // FFN_HALF)]
    up = [(jnp.dot(u, wa_ref[:, cols], preferred_element_type=F32),
           jnp.dot(u, wb_ref[:, cols], preferred_element_type=F32)) for cols in halves]
    contrib = None
    for cols, (ha, hb) in zip(halves, up):
        conv = _causal_conv_rows(ha, first, p_ref.at[:, cols], halo_ref.at[:, :, cols], j, cw_ref.at[:, cols])
        act = (_gelu_tanh(conv) * hb).astype(BF16)
        part = jnp.dot(act, wd_ref[cols, :], preferred_element_type=F32)
        contrib = part if contrib is None else contrib + part
    acc_ref[...] += contrib

    @pl.when(j == nf - 1)
    def _():
        r = alpha * x_ref[...] + (1.0 + gate_ref[...]) * acc_ref[...]
        o_ref[...] = _layer_norm_rows(r, g_ref[...], b_ref[...])


def _ffn_call(x1, mod4, wup, cw, wdown, g, b, seq, alpha):
    m, d = x1.shape
    dff = wdown.shape[0]
    nf = dff // TF_FFN
    tps = seq // TM_FFN
    row = lambda i, j: (i, 0)
    full = lambda i, j: (0, 0)
    return pl.pallas_call(
        functools.partial(_ffn_kernel, alpha, tps, nf),
        out_shape=jax.ShapeDtypeStruct((m, d), F32),
        grid=(m // TM_FFN, nf),
        in_specs=[pl.BlockSpec((TM_FFN, d), row),
                  pl.BlockSpec((None, None, 1, d), lambda i, j: (i // tps, 4, 0, 0)),
                  pl.BlockSpec((None, None, 1, d), lambda i, j: (i // tps, 3, 0, 0)),
                  pl.BlockSpec((None, None, 1, d), lambda i, j: (i // tps, 5, 0, 0)),
                  pl.BlockSpec((d, TF_FFN), lambda i, j: (0, j)),
                  pl.BlockSpec((d, TF_FFN), lambda i, j: (0, nf + j)),
                  pl.BlockSpec((SUBLANES, TF_FFN), lambda i, j: (0, j)),
                  pl.BlockSpec((TF_FFN, d), lambda i, j: (j, 0)),
                  pl.BlockSpec((1, d), full),
                  pl.BlockSpec((1, d), full)],
        out_specs=pl.BlockSpec((TM_FFN, d), row),
        scratch_shapes=[pltpu.VMEM((TM_FFN, d), BF16),
                        pltpu.VMEM((TM_FFN, d), F32),
                        pltpu.VMEM((TM_FFN + SUBLANES, TF_FFN), F32),
                        pltpu.VMEM((nf, SUBLANES, TF_FFN), F32)],
        compiler_params=_params(("arbitrary", "arbitrary")),
        name="ffn",
    )(x1, mod4, mod4, mod4, wup, wup, cw, wdown, g, b)


def _pad_rows(a, rows):
    return jnp.pad(a, ((0, rows - a.shape[0]), (0, 0)))


def kernel(x, c, w_cond, b_cond, w_in, conv_a, idx_kn_g, idx_kn_b, w_a, w_b, w_o, ln1_g, ln1_b,
           w_up, conv_f, w_down, ln2_g, ln2_b):
    batch, seq, d = x.shape
    depth = w_cond.shape[0]
    alpha = (2.0 * depth) ** 0.25
    d_attn = N_HEADS * HEAD_DIM
    d_kv = N_KV_HEADS * HEAD_DIM
    d_qi = IDX_HEADS * IDX_DIM
    d_conv = conv_a.shape[2]
    assert min(TOPK_MAX, seq // 4) == TOPK_MAX and seq % TM == 0 and TM % TK == 0 and TQ % (2 * TK) == 0
    o_q = 3 * d_conv
    o_k = o_q + d_attn
    o_v = o_k + d_kv
    o_qi = o_v + d_kv
    o_ki = o_qi + d_qi
    o_g = o_ki + IDX_DIM + IDX_HEADS
    cols = {"q": 0, "qi": d_attn, "k": d_attn + d_qi, "ga": d_attn + d_qi + d_kv,
            "gb": d_attn + d_qi + d_kv + d}
    q_scale = HEAD_DIM ** -0.5 * math.log2(math.e)
    colscale = jnp.concatenate([jnp.full((1, d_attn), q_scale, F32),
                                jnp.ones((1, d_qi + d_kv + 2 * d), F32)], axis=1)

    x2 = x.reshape(batch * seq, d)
    c_pad = _pad_rows(c, 2 * SUBLANES)
    for l in range(depth):
        mod = _mod_call(c_pad, w_cond[l], b_cond[l][None, :])[:batch]
        mod4 = mod.reshape(batch, N_MOD, 1, d)

        wl = w_in[l]
        w1 = wl[:, :o_q].astype(BF16)
        w2 = jnp.concatenate([wl[:, o_q:o_k], wl[:, o_qi:o_ki], wl[:, o_k:o_v], wl[:, o_g:]],
                             axis=1).astype(BF16)
        wvt = wl[:, o_v:o_qi].T.astype(BF16)
        wk = jnp.pad(wl[:, o_ki:o_g], ((0, 0), (0, LANES - (o_g - o_ki)))).astype(BF16)
        kg = jnp.pad(idx_kn_g[l], (0, LANES - IDX_DIM))[None, :]
        kb = jnp.pad(idx_kn_b[l], (0, LANES - IDX_DIM))[None, :]

        y_a = _conv_a_call(x2, mod4, w1, _pad_rows(conv_a[l], SUBLANES), seq)
        p2, ka, kbo, wit, vt = _proj_call(x2, mod4, w2, colscale, wk, kg, kb, wvt, seq)
        y_b = _attn_call(p2, ka, kbo, wit, vt, batch, seq, cols)
        merged = _merge_call(y_a, y_b, w_a[l].astype(BF16), w_b[l].astype(BF16), p2, cols)
        x2 = _out_call(merged, w_o[l].astype(BF16), x2, mod4, ln1_g[l][None, :], ln1_b[l][None, :],
                       seq, alpha)
        x2 = _ffn_call(x2, mod4, w_up[l].astype(BF16), _pad_rows(conv_f[l], SUBLANES),
                       w_down[l].astype(BF16), ln2_g[l][None, :], ln2_b[l][None, :], seq, alpha)
    return x2.reshape(batch, seq, d)
```

```python
import functools
import math

import jax
import jax.numpy as jnp
from jax import lax
from jax.experimental import pallas as pl
from jax.experimental.pallas import tpu as pltpu

F32 = jnp.float32
BF16 = jnp.bfloat16
I32 = jnp.int32

CHUNK = 64
N_HEADS = 16
N_KV_HEADS = 4
HEAD_DIM = 128
GROUP = N_HEADS // N_KV_HEADS
IDX_HEADS = 16
IDX_DIM = 64
IDX_W_SCALE = (IDX_HEADS * IDX_DIM) ** -0.5
TOPK_MAX = 256
LN_EPS = 1e-5
N_MOD = 6

LANES = 128
SUBLANES = 8
VMEM_LIMIT = 56 * 1024 * 1024

TM = 1024
TN_CONV = 512
CONV_HALF = 256
TN_PROJ = 1536
TN_MERGE = 512
TM_OUT = 512
TM_FFN = 512
TF_FFN = 512
FFN_HALF = 256
TQ = 512
TK = 256
ONES_ROWS = 16
V_ROWS = HEAD_DIM + ONES_ROWS

NEG_BIAS = -1e30
M_INIT = -3e38
INT_MIN = -(2 ** 31)

_NT = (((1,), (1,)), ((), ()))


def _params(sem):
    return pltpu.CompilerParams(dimension_semantics=sem, vmem_limit_bytes=VMEM_LIMIT)


def _sigmoid(x):
    return 1.0 / (1.0 + jnp.exp(-x))


def _layer_norm_rows(r, g, b):
    mu = jnp.mean(r, axis=-1, keepdims=True)
    d = r - mu
    var = jnp.mean(d * d, axis=-1, keepdims=True)
    return d * lax.rsqrt(var + LN_EPS) * g + b


def _mod_kernel(c_ref, w_ref, b_ref, o_ref):
    c = c_ref[...]
    ca = (c * _sigmoid(c)).astype(BF16)
    o_ref[...] = jnp.dot(ca, w_ref[...].astype(BF16), preferred_element_type=F32) + b_ref[...]


def _mod_call(c_pad, w_cond, b_cond):
    rows, d = c_pad.shape
    n = w_cond.shape[1]
    tn = 1024
    return pl.pallas_call(
        _mod_kernel,
        out_shape=jax.ShapeDtypeStruct((rows, n), F32),
        grid=(n // tn,),
        in_specs=[pl.BlockSpec((rows, d), lambda j: (0, 0)),
                  pl.BlockSpec((d, tn), lambda j: (0, j)),
                  pl.BlockSpec((1, tn), lambda j: (0, j))],
        out_specs=pl.BlockSpec((rows, tn), lambda j: (0, j)),
        compiler_params=_params(("arbitrary",)),
        name="mod",
    )(c_pad, w_cond, b_cond)


def _causal_conv_rows(p, first, halo_ref, j, cw_ref):
    tm = p.shape[0]
    halo = jnp.where(first, 0.0, halo_ref[j])
    halo_ref[j] = p[tm - SUBLANES:, :]
    ext = jnp.concatenate([halo, p], axis=0)
    prev1 = pltpu.roll(ext, 1, axis=0)[SUBLANES:, :]
    prev2 = pltpu.roll(ext, 2, axis=0)[SUBLANES:, :]
    return cw_ref[0:1, :] * prev2 + cw_ref[1:2, :] * prev1 + cw_ref[2:3, :] * p


def _conv_a_kernel(tiles_per_seq, x_ref, sc_ref, sh_ref, wb_ref, wc_ref, wh_ref, cw_ref,
                   o_ref, u_ref, halo_ref):
    i = pl.program_id(0)
    j = pl.program_id(1)

    @pl.when(j == 0)
    def _():
        u_ref[...] = (x_ref[...] * (1.0 + sc_ref[...]) + sh_ref[...]).astype(BF16)

    @pl.when((i == 0) & (j == 0))
    def _():
        halo_ref[...] = jnp.zeros(halo_ref.shape, F32)

    u = u_ref[...]
    first = (i % tiles_per_seq) == 0
    halves = [slice(h * CONV_HALF, (h + 1) * CONV_HALF) for h in range(TN_CONV // CONV_HALF)]
    proj = [tuple(jnp.dot(u, w_ref[:, cols], preferred_element_type=F32) for w_ref in (wb_ref, wc_ref, wh_ref))
            for cols in halves]
    for cols, (cb, cc, ch) in zip(halves, proj):
        conv = _causal_conv_rows(cc * ch, first, halo_ref.at[:, :, cols], j, cw_ref.at[:, cols])
        o_ref[:, cols] = (cb * conv).astype(BF16)


def _conv_a_call(x2, mod4, w1, dc, cw, seq):
    m, d = x2.shape
    nj = dc // TN_CONV
    tps = seq // TM
    row = lambda i, j: (i, 0)
    return pl.pallas_call(
        functools.partial(_conv_a_kernel, tps),
        out_shape=jax.ShapeDtypeStruct((m, dc), BF16),
        grid=(m // TM, nj),
        in_specs=[pl.BlockSpec((TM, d), row),
                  pl.BlockSpec((None, None, 1, d), lambda i, j: (i // tps, 1, 0, 0)),
                  pl.BlockSpec((None, None, 1, d), lambda i, j: (i // tps, 0, 0, 0)),
                  pl.BlockSpec((d, TN_CONV), lambda i, j: (0, j)),
                  pl.BlockSpec((d, TN_CONV), lambda i, j: (0, nj + j)),
                  pl.BlockSpec((d, TN_CONV), lambda i, j: (0, 2 * nj + j)),
                  pl.BlockSpec((SUBLANES, TN_CONV), lambda i, j: (0, j))],
        out_specs=pl.BlockSpec((TM, TN_CONV), lambda i, j: (i, j)),
        scratch_shapes=[pltpu.VMEM((TM, d), BF16),
                        pltpu.VMEM((nj, SUBLANES, TN_CONV), F32)],
        compiler_params=_params(("arbitrary", "arbitrary")),
        name="conv_a",
    )(x2, mod4, mod4, w1, w1, w1, cw)


def _proj_kernel(x_ref, sc_ref, sh_ref, w_ref, cs_ref, wk_ref, kg_ref, kb_ref, wvt_ref,
                 o_ref, ka_ref, kbo_ref, wit_ref, vt_ref, u_ref):
    j = pl.program_id(1)

    @pl.when(j == 0)
    def _():
        ub = (x_ref[...] * (1.0 + sc_ref[...]) + sh_ref[...]).astype(BF16)
        u_ref[...] = ub
        kw = jnp.dot(ub, wk_ref[...], preferred_element_type=F32)
        lane = lax.broadcasted_iota(I32, (1, LANES), 1)
        is_key = lane < IDX_DIM
        mu = jnp.sum(jnp.where(is_key, kw, 0.0), axis=-1, keepdims=True) * (1.0 / IDX_DIM)
        dk = jnp.where(is_key, kw - mu, 0.0)
        var = jnp.sum(dk * dk, axis=-1, keepdims=True) * (1.0 / IDX_DIM)
        kn = dk * lax.rsqrt(var + LN_EPS) * kg_ref[...] + kb_ref[...]
        ka_ref[...] = kn.astype(BF16)
        kbo_ref[...] = pltpu.roll(kn, IDX_DIM, axis=1).astype(BF16)
        wit_ref[...] = (kw * IDX_W_SCALE).T
        vt = lax.dot_general(wvt_ref[...], ub, _NT, preferred_element_type=F32).astype(BF16)
        ones = jnp.ones((ONES_ROWS, TK), BF16)
        for t in range(vt_ref.shape[0]):
            for n in range(N_KV_HEADS):
                vt_ref[t, n * V_ROWS:n * V_ROWS + HEAD_DIM, :] = (
                    vt[n * HEAD_DIM:(n + 1) * HEAD_DIM, t * TK:(t + 1) * TK])
                vt_ref[t, n * V_ROWS + HEAD_DIM:(n + 1) * V_ROWS, :] = ones

    acc = jnp.dot(u_ref[...], w_ref[...], preferred_element_type=F32)
    o_ref[...] = (acc * cs_ref[...]).astype(BF16)


def _proj_call(x2, mod4, w2, colscale, wk, kg, kb, wvt, seq):
    m, d = x2.shape
    n = w2.shape[1]
    d_kv = wvt.shape[0]
    tps = seq // TM
    small = lambda i, j: (i, 0)
    full = lambda i, j: (0, 0)
    return pl.pallas_call(
        _proj_kernel,
        out_shape=(jax.ShapeDtypeStruct((m, n), BF16),
                   jax.ShapeDtypeStruct((m, LANES), BF16),
                   jax.ShapeDtypeStruct((m, LANES), BF16),
                   jax.ShapeDtypeStruct((LANES, m), F32),
                   jax.ShapeDtypeStruct((m // TK, N_KV_HEADS * V_ROWS, TK), BF16)),
        grid=(m // TM, n // TN_PROJ),
        in_specs=[pl.BlockSpec((TM, d), small),
                  pl.BlockSpec((None, None, 1, d), lambda i, j: (i // tps, 1, 0, 0)),
                  pl.BlockSpec((None, None, 1, d), lambda i, j: (i // tps, 0, 0, 0)),
                  pl.BlockSpec((d, TN_PROJ), lambda i, j: (0, j)),
                  pl.BlockSpec((1, TN_PROJ), lambda i, j: (0, j)),
                  pl.BlockSpec((d, LANES), full),
                  pl.BlockSpec((1, LANES), full),
                  pl.BlockSpec((1, LANES), full),
                  pl.BlockSpec((d_kv, d), full)],
        out_specs=(pl.BlockSpec((TM, TN_PROJ), lambda i, j: (i, j)),
                   pl.BlockSpec((TM, LANES), small),
                   pl.BlockSpec((TM, LANES), small),
                   pl.BlockSpec((LANES, TM), lambda i, j: (0, i)),
                   pl.BlockSpec((TM // TK, N_KV_HEADS * V_ROWS, TK), lambda i, j: (i, 0, 0))),
        scratch_shapes=[pltpu.VMEM((TM, d), BF16)],
        compiler_params=_params(("arbitrary", "arbitrary")),
        name="proj",
    )(x2, mod4, mod4, w2, colscale, wk, kg, kb, wvt)


RED_ROWS = 32
CNT_ROWS = 16


def _attn_kernel(q_ref, k_ref, vt_ref, qi_ref, ka_ref, kb_ref, wit_ref, o_ref,
                 key_ref, sa_ref, sb_ref, ma_ref, mb_ref, p_ref, m_ref, acc_ref):
    i = pl.program_id(1)
    q0 = i * TQ
    n_kt = (q0 + TQ) // TK
    q_pos = q0 + lax.broadcasted_iota(I32, (1, TQ), 1)
    limit = (q_pos // CHUNK + 1) * CHUNK
    key_iota = lax.broadcasted_iota(I32, (TK, TQ), 0)

    def score_body(j, carry):
        ks = pl.multiple_of(j * TK, TK)
        ka = ka_ref[pl.ds(ks, TK), :]
        kb = kb_ref[pl.ds(ks, TK), :]
        acc = None
        for p in range(IDX_HEADS // 2):
            qp = qi_ref[:, p * LANES:(p + 1) * LANES]
            d0 = lax.dot_general(ka, qp, _NT, preferred_element_type=F32)
            d1 = lax.dot_general(kb, qp, _NT, preferred_element_type=F32)
            w0 = wit_ref[IDX_DIM + 2 * p:IDX_DIM + 2 * p + 1, :]
            w1 = wit_ref[IDX_DIM + 2 * p + 1:IDX_DIM + 2 * p + 2, :]
            t = w0 * jnp.maximum(d0, 0.0) + w1 * jnp.maximum(d1, 0.0)
            acc = t if acc is None else acc + t
        bits = pltpu.bitcast(acc, I32)
        key = bits ^ ((bits >> 31) & 0x7FFFFFFF)
        key_ref[j] = jnp.where(ks + key_iota < limit, key, INT_MIN)
        return carry

    lax.fori_loop(0, n_kt, score_body, 0)

    def count_ge(cand):
        def body(j, c):
            for r in range(TK // CNT_ROWS):
                c = c + jnp.where(key_ref[j, r * CNT_ROWS:(r + 1) * CNT_ROWS, :] >= cand, 1.0, 0.0)
            return c

        c = lax.fori_loop(0, n_kt, body, jnp.zeros((CNT_ROWS, TQ), F32))
        return jnp.sum(c, axis=0, keepdims=True)

    kf = float(TOPK_MAX)
    t0 = jnp.where(count_ge(jnp.zeros((1, TQ), I32)) >= kf, 0, INT_MIN).astype(I32)

    def bit_body(b, t):
        cand = t | lax.shift_left(jnp.int32(1), 30 - b)
        return jnp.where(count_ge(cand) >= kf, cand, t)

    thr = lax.fori_loop(0, 31, bit_body, t0)
    thr = jnp.maximum(thr, INT_MIN + 1)

    def bias_body(j, carry):
        bias = jnp.where(key_ref[j] >= thr, 0.0, NEG_BIAS).astype(F32)
        key_ref[j] = pltpu.bitcast(bias, I32)
        return carry

    lax.fori_loop(0, n_kt, bias_body, 0)

    n_ch = TK // RED_ROWS
    for n in range(N_KV_HEADS):
        qg = [q_ref[:, (n * GROUP + g) * HEAD_DIM:(n * GROUP + g + 1) * HEAD_DIM] for g in range(GROUP)]
        m_ref[...] = jnp.full(m_ref.shape, M_INIT, F32)
        acc_ref[...] = jnp.zeros(acc_ref.shape, F32)

        def logits(j, dst_ref, max_ref, n=n, qg=qg):
            ks = pl.multiple_of(j * TK, TK)
            kt = k_ref[pl.ds(ks, TK), n * HEAD_DIM:(n + 1) * HEAD_DIM]
            bias = pltpu.bitcast(key_ref[j], F32)
            for g in range(GROUP):
                s = lax.dot_general(kt, qg[g], _NT, preferred_element_type=F32) + bias
                dst_ref[g] = s
                part = jnp.max(s.reshape(n_ch, RED_ROWS, TQ), axis=0)
                max_ref[g] = jnp.max(part, axis=0, keepdims=True)

        def softmax_pv(j, src_ref, max_ref, slot, n=n):
            vt = vt_ref[j, n * V_ROWS:(n + 1) * V_ROWS, :]
            for g in range(GROUP):
                m_old = m_ref[g]
                m_new = jnp.maximum(m_old, max_ref[g])
                alpha = jnp.exp2(m_old - m_new)
                m_ref[g] = m_new
                for c in range(n_ch):
                    rows = slice(c * RED_ROWS, (c + 1) * RED_ROWS)
                    p_ref[slot, g, rows, :] = jnp.exp2((src_ref[g, rows, :] - m_new).astype(BF16))
                acc_ref[g] = alpha * acc_ref[g] + jnp.dot(vt, p_ref[slot, g], preferred_element_type=F32)

        def pair_body(jj, carry):
            j = 2 * jj
            logits(j, sa_ref, ma_ref)
            logits(j + 1, sb_ref, mb_ref)
            softmax_pv(j, sa_ref, ma_ref, 0)
            softmax_pv(j + 1, sb_ref, mb_ref, 1)
            return carry

        lax.fori_loop(0, n_kt // 2, pair_body, 0)
        for g in range(GROUP):
            h = n * GROUP + g
            o = acc_ref[g, 0:HEAD_DIM, :] / acc_ref[g, HEAD_DIM:HEAD_DIM + 1, :]
            o_ref[:, h * HEAD_DIM:(h + 1) * HEAD_DIM] = o.T.astype(BF16)


def _attn_call(p2, ka, kb, wit, vt, batch, seq, cols):
    d_attn = N_HEADS * HEAD_DIM
    d_kv = N_KV_HEADS * HEAD_DIM
    d_qi = IDX_HEADS * IDX_DIM
    nq = seq // TQ
    nkt = seq // TK
    once = pl.Buffered(1)
    return pl.pallas_call(
        _attn_kernel,
        out_shape=jax.ShapeDtypeStruct((batch * seq, d_attn), BF16),
        grid=(batch, nq),
        in_specs=[pl.BlockSpec((TQ, d_attn), lambda b, i: (b * nq + i, cols["q"] // d_attn)),
                  pl.BlockSpec((seq, d_kv), lambda b, i: (b, cols["k"] // d_kv), pipeline_mode=once),
                  pl.BlockSpec((nkt, N_KV_HEADS * V_ROWS, TK), lambda b, i: (b, 0, 0), pipeline_mode=once),
                  pl.BlockSpec((TQ, d_qi), lambda b, i: (b * nq + i, cols["qi"] // d_qi)),
                  pl.BlockSpec((seq, LANES), lambda b, i: (b, 0), pipeline_mode=once),
                  pl.BlockSpec((seq, LANES), lambda b, i: (b, 0), pipeline_mode=once),
                  pl.BlockSpec((LANES, TQ), lambda b, i: (0, b * nq + i))],
        out_specs=pl.BlockSpec((TQ, d_attn), lambda b, i: (b * nq + i, 0)),
        scratch_shapes=[pltpu.VMEM((nkt, TK, TQ), I32),
                        pltpu.VMEM((GROUP, TK, TQ), F32),
                        pltpu.VMEM((GROUP, TK, TQ), F32),
                        pltpu.VMEM((GROUP, 1, TQ), F32),
                        pltpu.VMEM((GROUP, 1, TQ), F32),
                        pltpu.VMEM((2, GROUP, TK, TQ), BF16),
                        pltpu.VMEM((GROUP, 1, TQ), F32),
                        pltpu.VMEM((GROUP, V_ROWS, TQ), F32)],
        compiler_params=_params(("arbitrary", "arbitrary")),
        name="attn",
    )(p2, p2, vt, p2, ka, kb, wit)


def _merge_kernel(ya_ref, yb_ref, wa_ref, wb_ref, ga_ref, gb_ref, o_ref):
    a = jnp.dot(ya_ref[...], wa_ref[...], preferred_element_type=F32)
    b = jnp.dot(yb_ref[...], wb_ref[...], preferred_element_type=F32)
    o_ref[...] = (_sigmoid(ga_ref[...].astype(F32)) * a
                  + _sigmoid(gb_ref[...].astype(F32)) * b).astype(BF16)


def _merge_call(ya, yb, wa, wb, p2, cols):
    m, d = ya.shape
    n = wa.shape[1]
    ga_blk = cols["ga"] // TN_MERGE
    gb_blk = cols["gb"] // TN_MERGE
    return pl.pallas_call(
        _merge_kernel,
        out_shape=jax.ShapeDtypeStruct((m, n), BF16),
        grid=(m // TM, n // TN_MERGE),
        in_specs=[pl.BlockSpec((TM, d), lambda i, j: (i, 0)),
                  pl.BlockSpec((TM, d), lambda i, j: (i, 0)),
                  pl.BlockSpec((d, TN_MERGE), lambda i, j: (0, j)),
                  pl.BlockSpec((d, TN_MERGE), lambda i, j: (0, j)),
                  pl.BlockSpec((TM, TN_MERGE), lambda i, j: (i, ga_blk + j)),
                  pl.BlockSpec((TM, TN_MERGE), lambda i, j: (i, gb_blk + j))],
        out_specs=pl.BlockSpec((TM, TN_MERGE), lambda i, j: (i, j)),
        compiler_params=_params(("arbitrary", "arbitrary")),
        name="merge",
    )(ya, yb, wa, wb, p2, p2)


def _out_kernel(alpha, mg_ref, wo_ref, x_ref, gate_ref, g_ref, b_ref, o_ref):
    mix = jnp.dot(mg_ref[...], wo_ref[...], preferred_element_type=F32)
    r = alpha * x_ref[...] + (1.0 + gate_ref[...]) * mix
    o_ref[...] = _layer_norm_rows(r, g_ref[...], b_ref[...])


def _out_call(merged, wo, x2, mod4, g, b, seq, alpha):
    m, d = x2.shape
    tps = seq // TM_OUT
    row = lambda i: (i, 0)
    full = lambda i: (0, 0)
    return pl.pallas_call(
        functools.partial(_out_kernel, alpha),
        out_shape=jax.ShapeDtypeStruct((m, d), F32),
        grid=(m // TM_OUT,),
        in_specs=[pl.BlockSpec((TM_OUT, merged.shape[1]), row),
                  pl.BlockSpec(wo.shape, full),
                  pl.BlockSpec((TM_OUT, d), row),
                  pl.BlockSpec((None, None, 1, d), lambda i: (i // tps, 2, 0, 0)),
                  pl.BlockSpec((1, d), full),
                  pl.BlockSpec((1, d), full)],
        out_specs=pl.BlockSpec((TM_OUT, d), row),
        compiler_params=_params(("arbitrary",)),
        name="out",
    )(merged, wo, x2, mod4, g, b)


def _gelu_tanh(x):
    return 0.5 * x * (1.0 + jnp.tanh(0.7978845608028654 * (x + 0.044715 * (x * x * x))))


def _ffn_kernel(alpha, tiles_per_seq, nf, x_ref, sc_ref, sh_ref, gate_ref, wa_ref, wb_ref, cw_ref,
                wd_ref, g_ref, b_ref, o_ref, u_ref, acc_ref, halo_ref):
    i = pl.program_id(0)
    j = pl.program_id(1)

    @pl.when(j == 0)
    def _():
        u_ref[...] = (x_ref[...] * (1.0 + sc_ref[...]) + sh_ref[...]).astype(BF16)
        acc_ref[...] = jnp.zeros(acc_ref.shape, F32)

    @pl.when((i == 0) & (j == 0))
    def _():
        halo_ref[...] = jnp.zeros(halo_ref.shape, F32)

    u = u_ref[...]
    first = (i % tiles_per_seq) == 0
    halves = [slice(h * FFN_HALF, (h + 1) * FFN_HALF) for h in range(TF_FFN // FFN_HALF)]
    up = [(jnp.dot(u, wa_ref[:, cols], preferred_element_type=F32),
           jnp.dot(u, wb_ref[:, cols], preferred_element_type=F32)) for cols in halves]
    contrib = None
    for cols, (ha, hb) in zip(halves, up):
        conv = _causal_conv_rows(ha, first, halo_ref.at[:, :, cols], j, cw_ref.at[:, cols])
        act = (_gelu_tanh(conv) * hb).astype(BF16)
        part = jnp.dot(act, wd_ref[cols, :], preferred_element_type=F32)
        contrib = part if contrib is None else contrib + part
    acc_ref[...] += contrib

    @pl.when(j == nf - 1)
    def _():
        r = alpha * x_ref[...] + (1.0 + gate_ref[...]) * acc_ref[...]
        o_ref[...] = _layer_norm_rows(r, g_ref[...], b_ref[...])


def _ffn_call(x1, mod4, wup, cw, wdown, g, b, seq, alpha):
    m, d = x1.shape
    dff = wdown.shape[0]
    nf = dff // TF_FFN
    tps = seq // TM_FFN
    row = lambda i, j: (i, 0)
    full = lambda i, j: (0, 0)
    return pl.pallas_call(
        functools.partial(_ffn_kernel, alpha, tps, nf),
        out_shape=jax.ShapeDtypeStruct((m, d), F32),
        grid=(m // TM_FFN, nf),
        in_specs=[pl.BlockSpec((TM_FFN, d), row),
                  pl.BlockSpec((None, None, 1, d), lambda i, j: (i // tps, 4, 0, 0)),
                  pl.BlockSpec((None, None, 1, d), lambda i, j: (i // tps, 3, 0, 0)),
                  pl.BlockSpec((None, None, 1, d), lambda i, j: (i // tps, 5, 0, 0)),
                  pl.BlockSpec((d, TF_FFN), lambda i, j: (0, j)),
                  pl.BlockSpec((d, TF_FFN), lambda i, j: (0, nf + j)),
                  pl.BlockSpec((SUBLANES, TF_FFN), lambda i, j: (0, j)),
                  pl.BlockSpec((TF_FFN, d), lambda i, j: (j, 0)),
                  pl.BlockSpec((1, d), full),
                  pl.BlockSpec((1, d), full)],
        out_specs=pl.BlockSpec((TM_FFN, d), row),
        scratch_shapes=[pltpu.VMEM((TM_FFN, d), BF16),
                        pltpu.VMEM((TM_FFN, d), F32),
                        pltpu.VMEM((nf, SUBLANES, TF_FFN), F32)],
        compiler_params=_params(("arbitrary", "arbitrary")),
        name="ffn",
    )(x1, mod4, mod4, mod4, wup, wup, cw, wdown, g, b)


def _pad_rows(a, rows):
    return jnp.pad(a, ((0, rows - a.shape[0]), (0, 0)))


def kernel(x, c, w_cond, b_cond, w_in, conv_a, idx_kn_g, idx_kn_b, w_a, w_b, w_o, ln1_g, ln1_b,
           w_up, conv_f, w_down, ln2_g, ln2_b):
    batch, seq, d = x.shape
    depth = w_cond.shape[0]
    alpha = (2.0 * depth) ** 0.25
    d_attn = N_HEADS * HEAD_DIM
    d_kv = N_KV_HEADS * HEAD_DIM
    d_qi = IDX_HEADS * IDX_DIM
    d_conv = conv_a.shape[2]
    assert min(TOPK_MAX, seq // 4) == TOPK_MAX and seq % TM == 0 and TM % TK == 0 and TQ % (2 * TK) == 0
    o_q = 3 * d_conv
    o_k = o_q + d_attn
    o_v = o_k + d_kv
    o_qi = o_v + d_kv
    o_ki = o_qi + d_qi
    o_g = o_ki + IDX_DIM + IDX_HEADS
    cols = {"q": 0, "qi": d_attn, "k": d_attn + d_qi, "ga": d_attn + d_qi + d_kv,
            "gb": d_attn + d_qi + d_kv + d}
    q_scale = HEAD_DIM ** -0.5 * math.log2(math.e)
    colscale = jnp.concatenate([jnp.full((1, d_attn), q_scale, F32),
                                jnp.ones((1, d_qi + d_kv + 2 * d), F32)], axis=1)

    x2 = x.reshape(batch * seq, d)
    c_pad = _pad_rows(c, 2 * SUBLANES)
    for l in range(depth):
        mod = _mod_call(c_pad, w_cond[l], b_cond[l][None, :])[:batch]
        mod4 = mod.reshape(batch, N_MOD, 1, d)

        wl = w_in[l].astype(BF16)
        w2 = jnp.concatenate([wl[:, o_q:o_k], wl[:, o_qi:o_ki], wl[:, o_k:o_v], wl[:, o_g:]], axis=1)
        wvt = wl[:, o_v:o_qi].T
        wk = jnp.pad(wl[:, o_ki:o_g], ((0, 0), (0, LANES - (o_g - o_ki))))
        kg = jnp.pad(idx_kn_g[l], (0, LANES - IDX_DIM))[None, :]
        kb = jnp.pad(idx_kn_b[l], (0, LANES - IDX_DIM))[None, :]

        y_a = _conv_a_call(x2, mod4, wl, d_conv, _pad_rows(conv_a[l], SUBLANES), seq)
        p2, ka, kbo, wit, vt = _proj_call(x2, mod4, w2, colscale, wk, kg, kb, wvt, seq)
        y_b = _attn_call(p2, ka, kbo, wit, vt, batch, seq, cols)
        merged = _merge_call(y_a, y_b, w_a[l].astype(BF16), w_b[l].astype(BF16), p2, cols)
        x2 = _out_call(merged, w_o[l].astype(BF16), x2, mod4, ln1_g[l][None, :], ln1_b[l][None, :],
                       seq, alpha)
        x2 = _ffn_call(x2, mod4, w_up[l].astype(BF16), _pad_rows(conv_f[l], SUBLANES),
                       w_down[l].astype(BF16), ln2_g[l][None, :], ln2_b[l][None, :], seq, alpha)
    return x2.reshape(batch, seq, d)
```

```python
import functools
import math

import jax
import jax.numpy as jnp
from jax import lax
from jax.experimental import pallas as pl
from jax.experimental.pallas import tpu as pltpu

F32 = jnp.float32
BF16 = jnp.bfloat16
I32 = jnp.int32

CHUNK = 64
N_HEADS = 16
N_KV_HEADS = 4
HEAD_DIM = 128
GROUP = N_HEADS // N_KV_HEADS
IDX_HEADS = 16
IDX_DIM = 64
IDX_W_SCALE = (IDX_HEADS * IDX_DIM) ** -0.5
TOPK_MAX = 256
LN_EPS = 1e-5
N_MOD = 6

LANES = 128
SUBLANES = 8
VMEM_LIMIT = 56 * 1024 * 1024

TM = 1024
TN_CONV = 512
CONV_HALF = 256
TN_PROJ = 1536
TM_MIX = 512
TN_MIX = 512
MIX_HALF = 256
REPACK_ROWS = 128
TM_FFN = 512
TF_FFN = 512
FFN_HALF = 256
TQ = 512
TK = 256
ONES_ROWS = 16
V_ROWS = HEAD_DIM + ONES_ROWS

NEG_BIAS = -1e30
M_INIT = -3e38
INT_MIN = -(2 ** 31)

_NT = (((1,), (1,)), ((), ()))


def _params(sem):
    return pltpu.CompilerParams(dimension_semantics=sem, vmem_limit_bytes=VMEM_LIMIT)


def _sigmoid(x):
    return 1.0 / (1.0 + jnp.exp(-x))


def _layer_norm_rows(r, g, b):
    mu = jnp.mean(r, axis=-1, keepdims=True)
    d = r - mu
    var = jnp.mean(d * d, axis=-1, keepdims=True)
    return d * lax.rsqrt(var + LN_EPS) * g + b


def _repack_kernel(offs, w_ref, w1_ref, w2_ref, wk_ref, wvt_ref):
    o_q, o_k, o_v, o_qi, o_ki, o_g, o_end = offs
    w1_ref[...] = w_ref[:, 0:o_q].astype(BF16)
    c = 0
    for lo, hi in ((o_q, o_k), (o_qi, o_ki), (o_k, o_v), (o_g, o_end)):
        w2_ref[:, c:c + hi - lo] = w_ref[:, lo:hi].astype(BF16)
        c += hi - lo
    lane = lax.broadcasted_iota(I32, (1, LANES), 1)
    wk_ref[...] = jnp.where(lane < o_g - o_ki, w_ref[:, o_ki:o_ki + LANES], 0.0).astype(BF16)
    wvt_ref[...] = w_ref[:, o_v:o_qi].T.astype(BF16)


def _repack_call(w_in, offs):
    d, d_in = w_in.shape
    o_q, o_k, o_v, o_qi, o_ki, o_g, o_end = offs
    n2 = (o_k - o_q) + (o_ki - o_qi) + (o_v - o_k) + (o_end - o_g)
    rows = REPACK_ROWS
    return pl.pallas_call(
        functools.partial(_repack_kernel, offs),
        out_shape=(jax.ShapeDtypeStruct((d, o_q), BF16),
                   jax.ShapeDtypeStruct((d, n2), BF16),
                   jax.ShapeDtypeStruct((d, LANES), BF16),
                   jax.ShapeDtypeStruct((o_qi - o_v, d), BF16)),
        grid=(d // rows,),
        in_specs=[pl.BlockSpec((rows, d_in), lambda i: (i, 0))],
        out_specs=(pl.BlockSpec((rows, o_q), lambda i: (i, 0)),
                   pl.BlockSpec((rows, n2), lambda i: (i, 0)),
                   pl.BlockSpec((rows, LANES), lambda i: (i, 0)),
                   pl.BlockSpec((o_qi - o_v, rows), lambda i: (0, i))),
        compiler_params=_params(("arbitrary",)),
        name="repack",
    )(w_in)


def _mod_kernel(c_ref, w_ref, b_ref, o_ref):
    c = c_ref[...]
    ca = (c * _sigmoid(c)).astype(BF16)
    o_ref[...] = jnp.dot(ca, w_ref[...].astype(BF16), preferred_element_type=F32) + b_ref[...]


def _mod_call(c_pad, w_cond, b_cond):
    rows, d = c_pad.shape
    n = w_cond.shape[1]
    tn = 1024
    return pl.pallas_call(
        _mod_kernel,
        out_shape=jax.ShapeDtypeStruct((rows, n), F32),
        grid=(n // tn,),
        in_specs=[pl.BlockSpec((rows, d), lambda j: (0, 0)),
                  pl.BlockSpec((d, tn), lambda j: (0, j)),
                  pl.BlockSpec((1, tn), lambda j: (0, j))],
        out_specs=pl.BlockSpec((rows, tn), lambda j: (0, j)),
        compiler_params=_params(("arbitrary",)),
        name="mod",
    )(c_pad, w_cond, b_cond)


def _causal_conv_rows(p, first, halo_ref, j, cw_ref):
    tm = p.shape[0]
    halo = jnp.where(first, 0.0, halo_ref[j])
    halo_ref[j] = p[tm - SUBLANES:, :]
    ext = jnp.concatenate([halo, p], axis=0)
    prev1 = pltpu.roll(ext, 1, axis=0)[SUBLANES:, :]
    prev2 = pltpu.roll(ext, 2, axis=0)[SUBLANES:, :]
    return cw_ref[0:1, :] * prev2 + cw_ref[1:2, :] * prev1 + cw_ref[2:3, :] * p


def _conv_a_kernel(tiles_per_seq, x_ref, sc_ref, sh_ref, wb_ref, wc_ref, wh_ref, cw_ref,
                   o_ref, u_ref, halo_ref):
    i = pl.program_id(0)
    j = pl.program_id(1)

    @pl.when(j == 0)
    def _():
        u_ref[...] = (x_ref[...] * (1.0 + sc_ref[...]) + sh_ref[...]).astype(BF16)

    @pl.when((i == 0) & (j == 0))
    def _():
        halo_ref[...] = jnp.zeros(halo_ref.shape, F32)

    u = u_ref[...]
    first = (i % tiles_per_seq) == 0
    halves = [slice(h * CONV_HALF, (h + 1) * CONV_HALF) for h in range(TN_CONV // CONV_HALF)]
    proj = [tuple(jnp.dot(u, w_ref[:, cols], preferred_element_type=F32) for w_ref in (wb_ref, wc_ref, wh_ref))
            for cols in halves]
    for cols, (cb, cc, ch) in zip(halves, proj):
        conv = _causal_conv_rows(cc * ch, first, halo_ref.at[:, :, cols], j, cw_ref.at[:, cols])
        o_ref[:, cols] = (cb * conv).astype(BF16)


def _conv_a_call(x2, mod4, w1, dc, cw, seq):
    m, d = x2.shape
    nj = dc // TN_CONV
    tps = seq // TM
    row = lambda i, j: (i, 0)
    return pl.pallas_call(
        functools.partial(_conv_a_kernel, tps),
        out_shape=jax.ShapeDtypeStruct((m, dc), BF16),
        grid=(m // TM, nj),
        in_specs=[pl.BlockSpec((TM, d), row),
                  pl.BlockSpec((None, None, 1, d), lambda i, j: (i // tps, 1, 0, 0)),
                  pl.BlockSpec((None, None, 1, d), lambda i, j: (i // tps, 0, 0, 0)),
                  pl.BlockSpec((d, TN_CONV), lambda i, j: (0, j)),
                  pl.BlockSpec((d, TN_CONV), lambda i, j: (0, nj + j)),
                  pl.BlockSpec((d, TN_CONV), lambda i, j: (0, 2 * nj + j)),
                  pl.BlockSpec((SUBLANES, TN_CONV), lambda i, j: (0, j))],
        out_specs=pl.BlockSpec((TM, TN_CONV), lambda i, j: (i, j)),
        scratch_shapes=[pltpu.VMEM((TM, d), BF16),
                        pltpu.VMEM((nj, SUBLANES, TN_CONV), F32)],
        compiler_params=_params(("arbitrary", "arbitrary")),
        name="conv_a",
    )(x2, mod4, mod4, w1, w1, w1, cw)


def _proj_kernel(x_ref, sc_ref, sh_ref, w_ref, cs_ref, wk_ref, kg_ref, kb_ref, wvt_ref,
                 o_ref, ka_ref, kbo_ref, wit_ref, vt_ref, u_ref):
    j = pl.program_id(1)

    @pl.when(j == 0)
    def _():
        ub = (x_ref[...] * (1.0 + sc_ref[...]) + sh_ref[...]).astype(BF16)
        u_ref[...] = ub
        kw = jnp.dot(ub, wk_ref[...], preferred_element_type=F32)
        lane = lax.broadcasted_iota(I32, (1, LANES), 1)
        is_key = lane < IDX_DIM
        mu = jnp.sum(jnp.where(is_key, kw, 0.0), axis=-1, keepdims=True) * (1.0 / IDX_DIM)
        dk = jnp.where(is_key, kw - mu, 0.0)
        var = jnp.sum(dk * dk, axis=-1, keepdims=True) * (1.0 / IDX_DIM)
        kn = dk * lax.rsqrt(var + LN_EPS) * kg_ref[...] + kb_ref[...]
        ka_ref[...] = kn.astype(BF16)
        kbo_ref[...] = pltpu.roll(kn, IDX_DIM, axis=1).astype(BF16)
        wit_ref[...] = (kw * IDX_W_SCALE).T
        vt = lax.dot_general(wvt_ref[...], ub, _NT, preferred_element_type=F32).astype(BF16)
        ones = jnp.ones((ONES_ROWS, TK), BF16)
        for t in range(vt_ref.shape[0]):
            for n in range(N_KV_HEADS):
                vt_ref[t, n * V_ROWS:n * V_ROWS + HEAD_DIM, :] = (
                    vt[n * HEAD_DIM:(n + 1) * HEAD_DIM, t * TK:(t + 1) * TK])
                vt_ref[t, n * V_ROWS + HEAD_DIM:(n + 1) * V_ROWS, :] = ones

    acc = jnp.dot(u_ref[...], w_ref[...], preferred_element_type=F32)
    o_ref[...] = (acc * cs_ref[...]).astype(BF16)


def _proj_call(x2, mod4, w2, colscale, wk, kg, kb, wvt, seq):
    m, d = x2.shape
    n = w2.shape[1]
    d_kv = wvt.shape[0]
    tps = seq // TM
    small = lambda i, j: (i, 0)
    full = lambda i, j: (0, 0)
    return pl.pallas_call(
        _proj_kernel,
        out_shape=(jax.ShapeDtypeStruct((m, n), BF16),
                   jax.ShapeDtypeStruct((m, LANES), BF16),
                   jax.ShapeDtypeStruct((m, LANES), BF16),
                   jax.ShapeDtypeStruct((LANES, m), F32),
                   jax.ShapeDtypeStruct((m // TK, N_KV_HEADS * V_ROWS, TK), BF16)),
        grid=(m // TM, n // TN_PROJ),
        in_specs=[pl.BlockSpec((TM, d), small),
                  pl.BlockSpec((None, None, 1, d), lambda i, j: (i // tps, 1, 0, 0)),
                  pl.BlockSpec((None, None, 1, d), lambda i, j: (i // tps, 0, 0, 0)),
                  pl.BlockSpec((d, TN_PROJ), lambda i, j: (0, j)),
                  pl.BlockSpec((1, TN_PROJ), lambda i, j: (0, j)),
                  pl.BlockSpec((d, LANES), full),
                  pl.BlockSpec((1, LANES), full),
                  pl.BlockSpec((1, LANES), full),
                  pl.BlockSpec((d_kv, d), full)],
        out_specs=(pl.BlockSpec((TM, TN_PROJ), lambda i, j: (i, j)),
                   pl.BlockSpec((TM, LANES), small),
                   pl.BlockSpec((TM, LANES), small),
                   pl.BlockSpec((LANES, TM), lambda i, j: (0, i)),
                   pl.BlockSpec((TM // TK, N_KV_HEADS * V_ROWS, TK), lambda i, j: (i, 0, 0))),
        scratch_shapes=[pltpu.VMEM((TM, d), BF16)],
        compiler_params=_params(("arbitrary", "arbitrary")),
        name="proj",
    )(x2, mod4, mod4, w2, colscale, wk, kg, kb, wvt)


RED_ROWS = 32
CNT_ROWS = 16


def _attn_kernel(q_ref, k_ref, vt_ref, qi_ref, ka_ref, kb_ref, wit_ref, o_ref,
                 key_ref, sa_ref, sb_ref, ma_ref, mb_ref, p_ref, m_ref, acc_ref):
    i = pl.program_id(1)
    q0 = i * TQ
    n_kt = (q0 + TQ) // TK
    q_pos = q0 + lax.broadcasted_iota(I32, (1, TQ), 1)
    limit = (q_pos // CHUNK + 1) * CHUNK
    key_iota = lax.broadcasted_iota(I32, (TK, TQ), 0)

    def score_body(j, carry):
        ks = pl.multiple_of(j * TK, TK)
        ka = ka_ref[pl.ds(ks, TK), :]
        kb = kb_ref[pl.ds(ks, TK), :]
        acc = None
        for p in range(IDX_HEADS // 2):
            qp = qi_ref[:, p * LANES:(p + 1) * LANES]
            d0 = lax.dot_general(ka, qp, _NT, preferred_element_type=F32)
            d1 = lax.dot_general(kb, qp, _NT, preferred_element_type=F32)
            w0 = wit_ref[IDX_DIM + 2 * p:IDX_DIM + 2 * p + 1, :]
            w1 = wit_ref[IDX_DIM + 2 * p + 1:IDX_DIM + 2 * p + 2, :]
            t = w0 * jnp.maximum(d0, 0.0) + w1 * jnp.maximum(d1, 0.0)
            acc = t if acc is None else acc + t
        bits = pltpu.bitcast(acc, I32)
        key = bits ^ ((bits >> 31) & 0x7FFFFFFF)
        key_ref[j] = jnp.where(ks + key_iota < limit, key, INT_MIN)
        return carry

    lax.fori_loop(0, n_kt, score_body, 0)

    def count_ge(cand):
        def body(j, c):
            for r in range(TK // CNT_ROWS):
                c = c + jnp.where(key_ref[j, r * CNT_ROWS:(r + 1) * CNT_ROWS, :] >= cand, 1.0, 0.0)
            return c

        c = lax.fori_loop(0, n_kt, body, jnp.zeros((CNT_ROWS, TQ), F32))
        return jnp.sum(c, axis=0, keepdims=True)

    kf = float(TOPK_MAX)
    t0 = jnp.where(count_ge(jnp.zeros((1, TQ), I32)) >= kf, 0, INT_MIN).astype(I32)

    def bit_body(b, t):
        cand = t | lax.shift_left(jnp.int32(1), 30 - b)
        return jnp.where(count_ge(cand) >= kf, cand, t)

    thr = lax.fori_loop(0, 31, bit_body, t0)
    thr = jnp.maximum(thr, INT_MIN + 1)

    def bias_body(j, carry):
        bias = jnp.where(key_ref[j] >= thr, 0.0, NEG_BIAS).astype(F32)
        key_ref[j] = pltpu.bitcast(bias, I32)
        return carry

    lax.fori_loop(0, n_kt, bias_body, 0)

    n_ch = TK // RED_ROWS
    for n in range(N_KV_HEADS):
        qg = [q_ref[:, (n * GROUP + g) * HEAD_DIM:(n * GROUP + g + 1) * HEAD_DIM] for g in range(GROUP)]
        m_ref[...] = jnp.full(m_ref.shape, M_INIT, F32)
        acc_ref[...] = jnp.zeros(acc_ref.shape, F32)

        def logits(j, dst_ref, max_ref, n=n, qg=qg):
            ks = pl.multiple_of(j * TK, TK)
            kt = k_ref[pl.ds(ks, TK), n * HEAD_DIM:(n + 1) * HEAD_DIM]
            bias = pltpu.bitcast(key_ref[j], F32)
            for g in range(GROUP):
                s = lax.dot_general(kt, qg[g], _NT, preferred_element_type=F32) + bias
                dst_ref[g] = s
                part = jnp.max(s.reshape(n_ch, RED_ROWS, TQ), axis=0)
                max_ref[g] = jnp.max(part, axis=0, keepdims=True)

        def softmax_pv(j, src_ref, max_ref, slot, n=n):
            vt = vt_ref[j, n * V_ROWS:(n + 1) * V_ROWS, :]
            for g in range(GROUP):
                m_old = m_ref[g]
                m_new = jnp.maximum(m_old, max_ref[g])
                alpha = jnp.exp2(m_old - m_new)
                m_ref[g] = m_new
                for c in range(n_ch):
                    rows = slice(c * RED_ROWS, (c + 1) * RED_ROWS)
                    p_ref[slot, g, rows, :] = jnp.exp2((src_ref[g, rows, :] - m_new).astype(BF16))
                acc_ref[g] = alpha * acc_ref[g] + jnp.dot(vt, p_ref[slot, g], preferred_element_type=F32)

        def pair_body(jj, carry):
            j = 2 * jj
            logits(j, sa_ref, ma_ref)
            logits(j + 1, sb_ref, mb_ref)
            softmax_pv(j, sa_ref, ma_ref, 0)
            softmax_pv(j + 1, sb_ref, mb_ref, 1)
            return carry

        lax.fori_loop(0, n_kt // 2, pair_body, 0)
        for g in range(GROUP):
            h = n * GROUP + g
            o = acc_ref[g, 0:HEAD_DIM, :] / acc_ref[g, HEAD_DIM:HEAD_DIM + 1, :]
            o_ref[:, h * HEAD_DIM:(h + 1) * HEAD_DIM] = o.T.astype(BF16)


def _attn_call(p2, ka, kb, wit, vt, batch, seq, cols):
    d_attn = N_HEADS * HEAD_DIM
    d_kv = N_KV_HEADS * HEAD_DIM
    d_qi = IDX_HEADS * IDX_DIM
    nq = seq // TQ
    nkt = seq // TK
    once = pl.Buffered(1)
    return pl.pallas_call(
        _attn_kernel,
        out_shape=jax.ShapeDtypeStruct((batch * seq, d_attn), BF16),
        grid=(batch, nq),
        in_specs=[pl.BlockSpec((TQ, d_attn), lambda b, i: (b * nq + i, cols["q"] // d_attn)),
                  pl.BlockSpec((seq, d_kv), lambda b, i: (b, cols["k"] // d_kv), pipeline_mode=once),
                  pl.BlockSpec((nkt, N_KV_HEADS * V_ROWS, TK), lambda b, i: (b, 0, 0), pipeline_mode=once),
                  pl.BlockSpec((TQ, d_qi), lambda b, i: (b * nq + i, cols["qi"] // d_qi)),
                  pl.BlockSpec((seq, LANES), lambda b, i: (b, 0), pipeline_mode=once),
                  pl.BlockSpec((seq, LANES), lambda b, i: (b, 0), pipeline_mode=once),
                  pl.BlockSpec((LANES, TQ), lambda b, i: (0, b * nq + i))],
        out_specs=pl.BlockSpec((TQ, d_attn), lambda b, i: (b * nq + i, 0)),
        scratch_shapes=[pltpu.VMEM((nkt, TK, TQ), I32),
                        pltpu.VMEM((GROUP, TK, TQ), F32),
                        pltpu.VMEM((GROUP, TK, TQ), F32),
                        pltpu.VMEM((GROUP, 1, TQ), F32),
                        pltpu.VMEM((GROUP, 1, TQ), F32),
                        pltpu.VMEM((2, GROUP, TK, TQ), BF16),
                        pltpu.VMEM((GROUP, 1, TQ), F32),
                        pltpu.VMEM((GROUP, V_ROWS, TQ), F32)],
        compiler_params=_params(("arbitrary", "arbitrary")),
        name="attn",
    )(p2, p2, vt, p2, ka, kb, wit)


def _mix_kernel(alpha, nj, ya_ref, yb_ref, wa_ref, wb_ref, ga_ref, gb_ref, wo_ref, x_ref, gate_ref,
                g_ref, b_ref, o_ref, acc_ref):
    j = pl.program_id(1)

    @pl.when(j == 0)
    def _():
        acc_ref[...] = jnp.zeros(acc_ref.shape, F32)

    ya = ya_ref[...]
    yb = yb_ref[...]
    halves = [slice(h * MIX_HALF, (h + 1) * MIX_HALF) for h in range(TN_MIX // MIX_HALF)]
    br = [(jnp.dot(ya, wa_ref[:, cols], preferred_element_type=F32),
           jnp.dot(yb, wb_ref[:, cols], preferred_element_type=F32)) for cols in halves]
    contrib = None
    for cols, (a, b) in zip(halves, br):
        merged = (_sigmoid(ga_ref[:, cols].astype(F32)) * a
                  + _sigmoid(gb_ref[:, cols].astype(F32)) * b).astype(BF16)
        part = jnp.dot(merged, wo_ref[cols, :], preferred_element_type=F32)
        contrib = part if contrib is None else contrib + part
    acc_ref[...] += contrib

    @pl.when(j == nj - 1)
    def _():
        r = alpha * x_ref[...] + (1.0 + gate_ref[...]) * acc_ref[...]
        o_ref[...] = _layer_norm_rows(r, g_ref[...], b_ref[...])


def _mix_call(ya, yb, wa, wb, p2, cols, wo, x2, mod4, g, b, seq, alpha):
    m, d = x2.shape
    dc = ya.shape[1]
    db = yb.shape[1]
    n = wa.shape[1]
    nj = n // TN_MIX
    ga_blk = cols["ga"] // TN_MIX
    gb_blk = cols["gb"] // TN_MIX
    tps = seq // TM_MIX
    row = lambda i, j: (i, 0)
    full = lambda i, j: (0, 0)
    return pl.pallas_call(
        functools.partial(_mix_kernel, alpha, nj),
        out_shape=jax.ShapeDtypeStruct((m, d), F32),
        grid=(m // TM_MIX, nj),
        in_specs=[pl.BlockSpec((TM_MIX, dc), row),
                  pl.BlockSpec((TM_MIX, db), row),
                  pl.BlockSpec((dc, TN_MIX), lambda i, j: (0, j)),
                  pl.BlockSpec((db, TN_MIX), lambda i, j: (0, j)),
                  pl.BlockSpec((TM_MIX, TN_MIX), lambda i, j: (i, ga_blk + j)),
                  pl.BlockSpec((TM_MIX, TN_MIX), lambda i, j: (i, gb_blk + j)),
                  pl.BlockSpec((TN_MIX, d), lambda i, j: (j, 0)),
                  pl.BlockSpec((TM_MIX, d), row),
                  pl.BlockSpec((None, None, 1, d), lambda i, j: (i // tps, 2, 0, 0)),
                  pl.BlockSpec((1, d), full),
                  pl.BlockSpec((1, d), full)],
        out_specs=pl.BlockSpec((TM_MIX, d), row),
        scratch_shapes=[pltpu.VMEM((TM_MIX, d), F32)],
        compiler_params=_params(("arbitrary", "arbitrary")),
        name="mix",
    )(ya, yb, wa, wb, p2, p2, wo, x2, mod4, g, b)


def _gelu_tanh(x):
    return 0.5 * x * (1.0 + jnp.tanh(0.7978845608028654 * (x + 0.044715 * (x * x * x))))


def _ffn_kernel(alpha, tiles_per_seq, nf, x_ref, sc_ref, sh_ref, gate_ref, wa_ref, wb_ref, cw_ref,
                wd_ref, g_ref, b_ref, o_ref, u_ref, acc_ref, halo_ref):
    i = pl.program_id(0)
    j = pl.program_id(1)

    @pl.when(j == 0)
    def _():
        u_ref[...] = (x_ref[...] * (1.0 + sc_ref[...]) + sh_ref[...]).astype(BF16)
        acc_ref[...] = jnp.zeros(acc_ref.shape, F32)

    @pl.when((i == 0) & (j == 0))
    def _():
        halo_ref[...] = jnp.zeros(halo_ref.shape, F32)

    u = u_ref[...]
    first = (i % tiles_per_seq) == 0
    halves = [slice(h * FFN_HALF, (h + 1) * FFN_HALF) for h in range(TF_FFN // FFN_HALF)]
    up = [(jnp.dot(u, wa_ref[:, cols], preferred_element_type=F32),
           jnp.dot(u, wb_ref[:, cols], preferred_element_type=F32)) for cols in halves]
    contrib = None
    for cols, (ha, hb) in zip(halves, up):
        conv = _causal_conv_rows(ha, first, halo_ref.at[:, :, cols], j, cw_ref.at[:, cols])
        act = (_gelu_tanh(conv) * hb).astype(BF16)
        part = jnp.dot(act, wd_ref[cols, :], preferred_element_type=F32)
        contrib = part if contrib is None else contrib + part
    acc_ref[...] += contrib

    @pl.when(j == nf - 1)
    def _():
        r = alpha * x_ref[...] + (1.0 + gate_ref[...]) * acc_ref[...]
        o_ref[...] = _layer_norm_rows(r, g_ref[...], b_ref[...])


def _ffn_call(x1, mod4, wup, cw, wdown, g, b, seq, alpha):
    m, d = x1.shape
    dff = wdown.shape[0]
    nf = dff // TF_FFN
    tps = seq // TM_FFN
    row = lambda i, j: (i, 0)
    full = lambda i, j: (0, 0)
    return pl.pallas_call(
        functools.partial(_ffn_kernel, alpha, tps, nf),
        out_shape=jax.ShapeDtypeStruct((m, d), F32),
        grid=(m // TM_FFN, nf),
        in_specs=[pl.BlockSpec((TM_FFN, d), row),
                  pl.BlockSpec((None, None, 1, d), lambda i, j: (i // tps, 4, 0, 0)),
                  pl.BlockSpec((None, None, 1, d), lambda i, j: (i // tps, 3, 0, 0)),
                  pl.BlockSpec((None, None, 1, d), lambda i, j: (i // tps, 5, 0, 0)),
                  pl.BlockSpec((d, TF_FFN), lambda i, j: (0, j)),
                  pl.BlockSpec((d, TF_FFN), lambda i, j: (0, nf + j)),
                  pl.BlockSpec((SUBLANES, TF_FFN), lambda i, j: (0, j)),
                  pl.BlockSpec((TF_FFN, d), lambda i, j: (j, 0)),
                  pl.BlockSpec((1, d), full),
                  pl.BlockSpec((1, d), full)],
        out_specs=pl.BlockSpec((TM_FFN, d), row),
        scratch_shapes=[pltpu.VMEM((TM_FFN, d), BF16),
                        pltpu.VMEM((TM_FFN, d), F32),
                        pltpu.VMEM((nf, SUBLANES, TF_FFN), F32)],
        compiler_params=_params(("arbitrary", "arbitrary")),
        name="ffn",
    )(x1, mod4, mod4, mod4, wup, wup, cw, wdown, g, b)


def _pad_rows(a, rows):
    return jnp.pad(a, ((0, rows - a.shape[0]), (0, 0)))


def kernel(x, c, w_cond, b_cond, w_in, conv_a, idx_kn_g, idx_kn_b, w_a, w_b, w_o, ln1_g, ln1_b,
           w_up, conv_f, w_down, ln2_g, ln2_b):
    batch, seq, d = x.shape
    depth = w_cond.shape[0]
    alpha = (2.0 * depth) ** 0.25
    d_attn = N_HEADS * HEAD_DIM
    d_kv = N_KV_HEADS * HEAD_DIM
    d_qi = IDX_HEADS * IDX_DIM
    d_conv = conv_a.shape[2]
    assert min(TOPK_MAX, seq // 4) == TOPK_MAX and seq % TM == 0 and TM % TK == 0 and TQ % (2 * TK) == 0
    o_q = 3 * d_conv
    o_k = o_q + d_attn
    o_v = o_k + d_kv
    o_qi = o_v + d_kv
    o_ki = o_qi + d_qi
    o_g = o_ki + IDX_DIM + IDX_HEADS
    cols = {"q": 0, "qi": d_attn, "k": d_attn + d_qi, "ga": d_attn + d_qi + d_kv,
            "gb": d_attn + d_qi + d_kv + d}
    q_scale = HEAD_DIM ** -0.5 * math.log2(math.e)
    colscale = jnp.concatenate([jnp.full((1, d_attn), q_scale, F32),
                                jnp.ones((1, d_qi + d_kv + 2 * d), F32)], axis=1)

    x2 = x.reshape(batch * seq, d)
    c_pad = _pad_rows(c, 2 * SUBLANES)
    for l in range(depth):
        mod = _mod_call(c_pad, w_cond[l], b_cond[l][None, :])[:batch]
        mod4 = mod.reshape(batch, N_MOD, 1, d)

        w1, w2, wk, wvt = _repack_call(w_in[l], (o_q, o_k, o_v, o_qi, o_ki, o_g, w_in.shape[2]))
        kg = jnp.pad(idx_kn_g[l], (0, LANES - IDX_DIM))[None, :]
        kb = jnp.pad(idx_kn_b[l], (0, LANES - IDX_DIM))[None, :]

        y_a = _conv_a_call(x2, mod4, w1, d_conv, _pad_rows(conv_a[l], SUBLANES), seq)
        p2, ka, kbo, wit, vt = _proj_call(x2, mod4, w2, colscale, wk, kg, kb, wvt, seq)
        y_b = _attn_call(p2, ka, kbo, wit, vt, batch, seq, cols)
        x2 = _mix_call(y_a, y_b, w_a[l].astype(BF16), w_b[l].astype(BF16), p2, cols, w_o[l].astype(BF16),
                       x2, mod4, ln1_g[l][None, :], ln1_b[l][None, :], seq, alpha)
        x2 = _ffn_call(x2, mod4, w_up[l].astype(BF16), _pad_rows(conv_f[l], SUBLANES),
                       w_down[l].astype(BF16), ln2_g[l][None, :], ln2_b[l][None, :], seq, alpha)
    return x2.reshape(batch, seq, d)
```

```python
import functools
import math

import jax
import jax.numpy as jnp
from jax import lax
from jax.experimental import pallas as pl
from jax.experimental.pallas import tpu as pltpu

F32 = jnp.float32
BF16 = jnp.bfloat16
I32 = jnp.int32

CHUNK = 64
N_HEADS = 16
N_KV_HEADS = 4
HEAD_DIM = 128
GROUP = N_HEADS // N_KV_HEADS
IDX_HEADS = 16
IDX_DIM = 64
IDX_W_SCALE = (IDX_HEADS * IDX_DIM) ** -0.5
TOPK_MAX = 256
LN_EPS = 1e-5
N_MOD = 6

LANES = 128
SUBLANES = 8
VMEM_LIMIT = 56 * 1024 * 1024

TM = 1024
TN_CONV = 512
CONV_HALF = 256
TN_PROJ = 1536
TM_MIX = 512
TN_MIX = 512
MIX_HALF = 256
REPACK_ROWS = 512
TM_FFN = 512
TF_FFN = 512
FFN_HALF = 256
TQ = 512
TK = 256
ONES_ROWS = 16
V_ROWS = HEAD_DIM + ONES_ROWS

NEG_BIAS = -1e30
M_INIT = -3e38
INT_MIN = -(2 ** 31)

_NT = (((1,), (1,)), ((), ()))


def _params(sem):
    return pltpu.CompilerParams(dimension_semantics=sem, vmem_limit_bytes=VMEM_LIMIT)


def _sigmoid(x):
    return 1.0 / (1.0 + jnp.exp(-x))


def _layer_norm_rows(r, g, b):
    mu = jnp.mean(r, axis=-1, keepdims=True)
    d = r - mu
    var = jnp.mean(d * d, axis=-1, keepdims=True)
    return d * lax.rsqrt(var + LN_EPS) * g + b


def _repack_kernel(src_ref, w_ref, o_ref):
    del src_ref
    o_ref[...] = w_ref[...].astype(BF16)


def _repack_call(wt, src_rows):
    d = wt.shape[1]
    nb = len(src_rows)
    return pl.pallas_call(
        _repack_kernel,
        out_shape=jax.ShapeDtypeStruct((nb * REPACK_ROWS, d), BF16),
        grid_spec=pltpu.PrefetchScalarGridSpec(
            num_scalar_prefetch=1,
            grid=(nb,),
            in_specs=[pl.BlockSpec((pl.Element(REPACK_ROWS), pl.Element(d)),
                                   lambda b, src: (pl.multiple_of(src[b], SUBLANES), 0))],
            out_specs=pl.BlockSpec((REPACK_ROWS, d), lambda b, src: (b, 0))),
        compiler_params=_params(("arbitrary",)),
        name="repack",
    )(jnp.asarray(src_rows, I32), wt)


def _mod_kernel(c_ref, w_ref, b_ref, o_ref):
    c = c_ref[...]
    ca = (c * _sigmoid(c)).astype(BF16)
    o_ref[...] = jnp.dot(ca, w_ref[...].astype(BF16), preferred_element_type=F32) + b_ref[...]


def _mod_call(c_pad, w_cond, b_cond):
    rows, d = c_pad.shape
    n = w_cond.shape[1]
    tn = 1024
    return pl.pallas_call(
        _mod_kernel,
        out_shape=jax.ShapeDtypeStruct((rows, n), F32),
        grid=(n // tn,),
        in_specs=[pl.BlockSpec((rows, d), lambda j: (0, 0)),
                  pl.BlockSpec((d, tn), lambda j: (0, j)),
                  pl.BlockSpec((1, tn), lambda j: (0, j))],
        out_specs=pl.BlockSpec((rows, tn), lambda j: (0, j)),
        compiler_params=_params(("arbitrary",)),
        name="mod",
    )(c_pad, w_cond, b_cond)


def _causal_conv_rows(p, first, halo_ref, j, cw_ref):
    tm = p.shape[0]
    halo = jnp.where(first, 0.0, halo_ref[j])
    halo_ref[j] = p[tm - SUBLANES:, :]
    ext = jnp.concatenate([halo, p], axis=0)
    prev1 = pltpu.roll(ext, 1, axis=0)[SUBLANES:, :]
    prev2 = pltpu.roll(ext, 2, axis=0)[SUBLANES:, :]
    return cw_ref[0:1, :] * prev2 + cw_ref[1:2, :] * prev1 + cw_ref[2:3, :] * p


def _conv_a_kernel(tiles_per_seq, x_ref, sc_ref, sh_ref, wb_ref, wc_ref, wh_ref, cw_ref,
                   o_ref, u_ref, halo_ref):
    i = pl.program_id(0)
    j = pl.program_id(1)

    @pl.when(j == 0)
    def _():
        u_ref[...] = (x_ref[...] * (1.0 + sc_ref[...]) + sh_ref[...]).astype(BF16)

    @pl.when((i == 0) & (j == 0))
    def _():
        halo_ref[...] = jnp.zeros(halo_ref.shape, F32)

    u = u_ref[...]
    first = (i % tiles_per_seq) == 0
    halves = [slice(h * CONV_HALF, (h + 1) * CONV_HALF) for h in range(TN_CONV // CONV_HALF)]
    proj = [tuple(lax.dot_general(u, w_ref[cols, :], _NT, preferred_element_type=F32)
                  for w_ref in (wb_ref, wc_ref, wh_ref)) for cols in halves]
    for cols, (cb, cc, ch) in zip(halves, proj):
        conv = _causal_conv_rows(cc * ch, first, halo_ref.at[:, :, cols], j, cw_ref.at[:, cols])
        o_ref[:, cols] = (cb * conv).astype(BF16)


def _conv_a_call(x2, mod4, wp, dc, cw, seq):
    m, d = x2.shape
    nj = dc // TN_CONV
    tps = seq // TM
    row = lambda i, j: (i, 0)
    return pl.pallas_call(
        functools.partial(_conv_a_kernel, tps),
        out_shape=jax.ShapeDtypeStruct((m, dc), BF16),
        grid=(m // TM, nj),
        in_specs=[pl.BlockSpec((TM, d), row),
                  pl.BlockSpec((None, None, 1, d), lambda i, j: (i // tps, 1, 0, 0)),
                  pl.BlockSpec((None, None, 1, d), lambda i, j: (i // tps, 0, 0, 0)),
                  pl.BlockSpec((TN_CONV, d), lambda i, j: (j, 0)),
                  pl.BlockSpec((TN_CONV, d), lambda i, j: (nj + j, 0)),
                  pl.BlockSpec((TN_CONV, d), lambda i, j: (2 * nj + j, 0)),
                  pl.BlockSpec((SUBLANES, TN_CONV), lambda i, j: (0, j))],
        out_specs=pl.BlockSpec((TM, TN_CONV), lambda i, j: (i, j)),
        scratch_shapes=[pltpu.VMEM((TM, d), BF16),
                        pltpu.VMEM((nj, SUBLANES, TN_CONV), F32)],
        compiler_params=_params(("arbitrary", "arbitrary")),
        name="conv_a",
    )(x2, mod4, mod4, wp, wp, wp, cw)


def _proj_kernel(x_ref, sc_ref, sh_ref, w_ref, cs_ref, wk_ref, kg_ref, kb_ref, wvt_ref,
                 o_ref, ka_ref, kbo_ref, wit_ref, vt_ref, u_ref):
    j = pl.program_id(1)

    @pl.when(j == 0)
    def _():
        ub = (x_ref[...] * (1.0 + sc_ref[...]) + sh_ref[...]).astype(BF16)
        u_ref[...] = ub
        kw = lax.dot_general(ub, wk_ref[...], _NT, preferred_element_type=F32)
        lane = lax.broadcasted_iota(I32, (1, LANES), 1)
        is_key = lane < IDX_DIM
        mu = jnp.sum(jnp.where(is_key, kw, 0.0), axis=-1, keepdims=True) * (1.0 / IDX_DIM)
        dk = jnp.where(is_key, kw - mu, 0.0)
        var = jnp.sum(dk * dk, axis=-1, keepdims=True) * (1.0 / IDX_DIM)
        kn = dk * lax.rsqrt(var + LN_EPS) * kg_ref[...] + kb_ref[...]
        ka_ref[...] = kn.astype(BF16)
        kbo_ref[...] = pltpu.roll(kn, IDX_DIM, axis=1).astype(BF16)
        wit_ref[...] = (kw * IDX_W_SCALE).T
        vt = lax.dot_general(wvt_ref[...], ub, _NT, preferred_element_type=F32).astype(BF16)
        ones = jnp.ones((ONES_ROWS, TK), BF16)
        for t in range(vt_ref.shape[0]):
            for n in range(N_KV_HEADS):
                vt_ref[t, n * V_ROWS:n * V_ROWS + HEAD_DIM, :] = (
                    vt[n * HEAD_DIM:(n + 1) * HEAD_DIM, t * TK:(t + 1) * TK])
                vt_ref[t, n * V_ROWS + HEAD_DIM:(n + 1) * V_ROWS, :] = ones

    acc = lax.dot_general(u_ref[...], w_ref[...], _NT, preferred_element_type=F32)
    o_ref[...] = (acc * cs_ref[...]).astype(BF16)


def _proj_call(x2, mod4, wp, row_main, n, row_v, d_kv, colscale, wk, kg, kb, seq):
    m, d = x2.shape
    main_blk = row_main // TN_PROJ
    v_blk = row_v // d_kv
    tps = seq // TM
    small = lambda i, j: (i, 0)
    full = lambda i, j: (0, 0)
    return pl.pallas_call(
        _proj_kernel,
        out_shape=(jax.ShapeDtypeStruct((m, n), BF16),
                   jax.ShapeDtypeStruct((m, LANES), BF16),
                   jax.ShapeDtypeStruct((m, LANES), BF16),
                   jax.ShapeDtypeStruct((LANES, m), F32),
                   jax.ShapeDtypeStruct((m // TK, N_KV_HEADS * V_ROWS, TK), BF16)),
        grid=(m // TM, n // TN_PROJ),
        in_specs=[pl.BlockSpec((TM, d), small),
                  pl.BlockSpec((None, None, 1, d), lambda i, j: (i // tps, 1, 0, 0)),
                  pl.BlockSpec((None, None, 1, d), lambda i, j: (i // tps, 0, 0, 0)),
                  pl.BlockSpec((TN_PROJ, d), lambda i, j: (main_blk + j, 0)),
                  pl.BlockSpec((1, TN_PROJ), lambda i, j: (0, j)),
                  pl.BlockSpec((LANES, d), full),
                  pl.BlockSpec((1, LANES), full),
                  pl.BlockSpec((1, LANES), full),
                  pl.BlockSpec((d_kv, d), lambda i, j: (v_blk, 0))],
        out_specs=(pl.BlockSpec((TM, TN_PROJ), lambda i, j: (i, j)),
                   pl.BlockSpec((TM, LANES), small),
                   pl.BlockSpec((TM, LANES), small),
                   pl.BlockSpec((LANES, TM), lambda i, j: (0, i)),
                   pl.BlockSpec((TM // TK, N_KV_HEADS * V_ROWS, TK), lambda i, j: (i, 0, 0))),
        scratch_shapes=[pltpu.VMEM((TM, d), BF16)],
        compiler_params=_params(("arbitrary", "arbitrary")),
        name="proj",
    )(x2, mod4, mod4, wp, colscale, wk, kg, kb, wp)


RED_ROWS = 32
CNT_ROWS = 16


def _attn_kernel(q_ref, k_ref, vt_ref, qi_ref, ka_ref, kb_ref, wit_ref, o_ref,
                 key_ref, sa_ref, sb_ref, ma_ref, mb_ref, p_ref, m_ref, acc_ref):
    i = pl.program_id(1)
    q0 = i * TQ
    n_kt = (q0 + TQ) // TK
    q_pos = q0 + lax.broadcasted_iota(I32, (1, TQ), 1)
    limit = (q_pos // CHUNK + 1) * CHUNK
    key_iota = lax.broadcasted_iota(I32, (TK, TQ), 0)

    def score_body(j, carry):
        ks = pl.multiple_of(j * TK, TK)
        ka = ka_ref[pl.ds(ks, TK), :]
        kb = kb_ref[pl.ds(ks, TK), :]
        acc = None
        for p in range(IDX_HEADS // 2):
            qp = qi_ref[:, p * LANES:(p + 1) * LANES]
            d0 = lax.dot_general(ka, qp, _NT, preferred_element_type=F32)
            d1 = lax.dot_general(kb, qp, _NT, preferred_element_type=F32)
            w0 = wit_ref[IDX_DIM + 2 * p:IDX_DIM + 2 * p + 1, :]
            w1 = wit_ref[IDX_DIM + 2 * p + 1:IDX_DIM + 2 * p + 2, :]
            t = w0 * jnp.maximum(d0, 0.0) + w1 * jnp.maximum(d1, 0.0)
            acc = t if acc is None else acc + t
        bits = pltpu.bitcast(acc, I32)
        key = bits ^ ((bits >> 31) & 0x7FFFFFFF)
        key_ref[j] = jnp.where(ks + key_iota < limit, key, INT_MIN)
        return carry

    lax.fori_loop(0, n_kt, score_body, 0)

    def count_ge(cand):
        def body(j, c):
            for r in range(TK // CNT_ROWS):
                c = c + jnp.where(key_ref[j, r * CNT_ROWS:(r + 1) * CNT_ROWS, :] >= cand, 1.0, 0.0)
            return c

        c = lax.fori_loop(0, n_kt, body, jnp.zeros((CNT_ROWS, TQ), F32))
        return jnp.sum(c, axis=0, keepdims=True)

    kf = float(TOPK_MAX)
    t0 = jnp.where(count_ge(jnp.zeros((1, TQ), I32)) >= kf, 0, INT_MIN).astype(I32)

    def bit_body(b, t):
        cand = t | lax.shift_left(jnp.int32(1), 30 - b)
        return jnp.where(count_ge(cand) >= kf, cand, t)

    thr = lax.fori_loop(0, 31, bit_body, t0)
    thr = jnp.maximum(thr, INT_MIN + 1)

    def bias_body(j, carry):
        bias = jnp.where(key_ref[j] >= thr, 0.0, NEG_BIAS).astype(F32)
        key_ref[j] = pltpu.bitcast(bias, I32)
        return carry

    lax.fori_loop(0, n_kt, bias_body, 0)

    n_ch = TK // RED_ROWS
    for n in range(N_KV_HEADS):
        qg = [q_ref[:, (n * GROUP + g) * HEAD_DIM:(n * GROUP + g + 1) * HEAD_DIM] for g in range(GROUP)]
        m_ref[...] = jnp.full(m_ref.shape, M_INIT, F32)
        acc_ref[...] = jnp.zeros(acc_ref.shape, F32)

        def logits(j, dst_ref, max_ref, n=n, qg=qg):
            ks = pl.multiple_of(j * TK, TK)
            kt = k_ref[pl.ds(ks, TK), n * HEAD_DIM:(n + 1) * HEAD_DIM]
            bias = pltpu.bitcast(key_ref[j], F32)
            for g in range(GROUP):
                s = lax.dot_general(kt, qg[g], _NT, preferred_element_type=F32) + bias
                dst_ref[g] = s
                part = jnp.max(s.reshape(n_ch, RED_ROWS, TQ), axis=0)
                max_ref[g] = jnp.max(part, axis=0, keepdims=True)

        def softmax_pv(j, src_ref, max_ref, slot, n=n):
            vt = vt_ref[j, n * V_ROWS:(n + 1) * V_ROWS, :]
            for g in range(GROUP):
                m_old = m_ref[g]
                m_new = jnp.maximum(m_old, max_ref[g])
                alpha = jnp.exp2(m_old - m_new)
                m_ref[g] = m_new
                for c in range(n_ch):
                    rows = slice(c * RED_ROWS, (c + 1) * RED_ROWS)
                    p_ref[slot, g, rows, :] = jnp.exp2((src_ref[g, rows, :] - m_new).astype(BF16))
                acc_ref[g] = alpha * acc_ref[g] + jnp.dot(vt, p_ref[slot, g], preferred_element_type=F32)

        def pair_body(jj, carry):
            j = 2 * jj
            logits(j, sa_ref, ma_ref)
            logits(j + 1, sb_ref, mb_ref)
            softmax_pv(j, sa_ref, ma_ref, 0)
            softmax_pv(j + 1, sb_ref, mb_ref, 1)
            return carry

        lax.fori_loop(0, n_kt // 2, pair_body, 0)
        for g in range(GROUP):
            h = n * GROUP + g
            o = acc_ref[g, 0:HEAD_DIM, :] / acc_ref[g, HEAD_DIM:HEAD_DIM + 1, :]
            o_ref[:, h * HEAD_DIM:(h + 1) * HEAD_DIM] = o.T.astype(BF16)


def _attn_call(p2, ka, kb, wit, vt, batch, seq, cols):
    d_attn = N_HEADS * HEAD_DIM
    d_kv = N_KV_HEADS * HEAD_DIM
    d_qi = IDX_HEADS * IDX_DIM
    nq = seq // TQ
    nkt = seq // TK
    once = pl.Buffered(1)
    return pl.pallas_call(
        _attn_kernel,
        out_shape=jax.ShapeDtypeStruct((batch * seq, d_attn), BF16),
        grid=(batch, nq),
        in_specs=[pl.BlockSpec((TQ, d_attn), lambda b, i: (b * nq + i, cols["q"] // d_attn)),
                  pl.BlockSpec((seq, d_kv), lambda b, i: (b, cols["k"] // d_kv), pipeline_mode=once),
                  pl.BlockSpec((nkt, N_KV_HEADS * V_ROWS, TK), lambda b, i: (b, 0, 0), pipeline_mode=once),
                  pl.BlockSpec((TQ, d_qi), lambda b, i: (b * nq + i, cols["qi"] // d_qi)),
                  pl.BlockSpec((seq, LANES), lambda b, i: (b, 0), pipeline_mode=once),
                  pl.BlockSpec((seq, LANES), lambda b, i: (b, 0), pipeline_mode=once),
                  pl.BlockSpec((LANES, TQ), lambda b, i: (0, b * nq + i))],
        out_specs=pl.BlockSpec((TQ, d_attn), lambda b, i: (b * nq + i, 0)),
        scratch_shapes=[pltpu.VMEM((nkt, TK, TQ), I32),
                        pltpu.VMEM((GROUP, TK, TQ), F32),
                        pltpu.VMEM((GROUP, TK, TQ), F32),
                        pltpu.VMEM((GROUP, 1, TQ), F32),
                        pltpu.VMEM((GROUP, 1, TQ), F32),
                        pltpu.VMEM((2, GROUP, TK, TQ), BF16),
                        pltpu.VMEM((GROUP, 1, TQ), F32),
                        pltpu.VMEM((GROUP, V_ROWS, TQ), F32)],
        compiler_params=_params(("arbitrary", "arbitrary")),
        name="attn",
    )(p2, p2, vt, p2, ka, kb, wit)


def _mix_kernel(alpha, nj, ya_ref, yb_ref, wa_ref, wb_ref, ga_ref, gb_ref, wo_ref, x_ref, gate_ref,
                g_ref, b_ref, o_ref, acc_ref):
    j = pl.program_id(1)

    @pl.when(j == 0)
    def _():
        acc_ref[...] = jnp.zeros(acc_ref.shape, F32)

    ya = ya_ref[...]
    yb = yb_ref[...]
    halves = [slice(h * MIX_HALF, (h + 1) * MIX_HALF) for h in range(TN_MIX // MIX_HALF)]
    br = [(jnp.dot(ya, wa_ref[:, cols], preferred_element_type=F32),
           jnp.dot(yb, wb_ref[:, cols], preferred_element_type=F32)) for cols in halves]
    contrib = None
    for cols, (a, b) in zip(halves, br):
        merged = (_sigmoid(ga_ref[:, cols].astype(F32)) * a
                  + _sigmoid(gb_ref[:, cols].astype(F32)) * b).astype(BF16)
        part = jnp.dot(merged, wo_ref[cols, :], preferred_element_type=F32)
        contrib = part if contrib is None else contrib + part
    acc_ref[...] += contrib

    @pl.when(j == nj - 1)
    def _():
        r = alpha * x_ref[...] + (1.0 + gate_ref[...]) * acc_ref[...]
        o_ref[...] = _layer_norm_rows(r, g_ref[...], b_ref[...])


def _mix_call(ya, yb, wa, wb, p2, cols, wo, x2, mod4, g, b, seq, alpha):
    m, d = x2.shape
    dc = ya.shape[1]
    db = yb.shape[1]
    n = wa.shape[1]
    nj = n // TN_MIX
    ga_blk = cols["ga"] // TN_MIX
    gb_blk = cols["gb"] // TN_MIX
    tps = seq // TM_MIX
    row = lambda i, j: (i, 0)
    full = lambda i, j: (0, 0)
    return pl.pallas_call(
        functools.partial(_mix_kernel, alpha, nj),
        out_shape=jax.ShapeDtypeStruct((m, d), F32),
        grid=(m // TM_MIX, nj),
        in_specs=[pl.BlockSpec((TM_MIX, dc), row),
                  pl.BlockSpec((TM_MIX, db), row),
                  pl.BlockSpec((dc, TN_MIX), lambda i, j: (0, j)),
                  pl.BlockSpec((db, TN_MIX), lambda i, j: (0, j)),
                  pl.BlockSpec((TM_MIX, TN_MIX), lambda i, j: (i, ga_blk + j)),
                  pl.BlockSpec((TM_MIX, TN_MIX), lambda i, j: (i, gb_blk + j)),
                  pl.BlockSpec((TN_MIX, d), lambda i, j: (j, 0)),
                  pl.BlockSpec((TM_MIX, d), row),
                  pl.BlockSpec((None, None, 1, d), lambda i, j: (i // tps, 2, 0, 0)),
                  pl.BlockSpec((1, d), full),
                  pl.BlockSpec((1, d), full)],
        out_specs=pl.BlockSpec((TM_MIX, d), row),
        scratch_shapes=[pltpu.VMEM((TM_MIX, d), F32)],
        compiler_params=_params(("arbitrary", "arbitrary")),
        name="mix",
    )(ya, yb, wa, wb, p2, p2, wo, x2, mod4, g, b)


def _gelu_tanh(x):
    return 0.5 * x * (1.0 + jnp.tanh(0.7978845608028654 * (x + 0.044715 * (x * x * x))))


def _ffn_kernel(alpha, tiles_per_seq, nf, x_ref, sc_ref, sh_ref, gate_ref, wa_ref, wb_ref, cw_ref,
                wd_ref, g_ref, b_ref, o_ref, u_ref, acc_ref, halo_ref):
    i = pl.program_id(0)
    j = pl.program_id(1)

    @pl.when(j == 0)
    def _():
        u_ref[...] = (x_ref[...] * (1.0 + sc_ref[...]) + sh_ref[...]).astype(BF16)
        acc_ref[...] = jnp.zeros(acc_ref.shape, F32)

    @pl.when((i == 0) & (j == 0))
    def _():
        halo_ref[...] = jnp.zeros(halo_ref.shape, F32)

    u = u_ref[...]
    first = (i % tiles_per_seq) == 0
    halves = [slice(h * FFN_HALF, (h + 1) * FFN_HALF) for h in range(TF_FFN // FFN_HALF)]
    up = [(jnp.dot(u, wa_ref[:, cols], preferred_element_type=F32),
           jnp.dot(u, wb_ref[:, cols], preferred_element_type=F32)) for cols in halves]
    contrib = None
    for cols, (ha, hb) in zip(halves, up):
        conv = _causal_conv_rows(ha, first, halo_ref.at[:, :, cols], j, cw_ref.at[:, cols])
        act = (_gelu_tanh(conv) * hb).astype(BF16)
        part = jnp.dot(act, wd_ref[cols, :], preferred_element_type=F32)
        contrib = part if contrib is None else contrib + part
    acc_ref[...] += contrib

    @pl.when(j == nf - 1)
    def _():
        r = alpha * x_ref[...] + (1.0 + gate_ref[...]) * acc_ref[...]
        o_ref[...] = _layer_norm_rows(r, g_ref[...], b_ref[...])


def _ffn_call(x1, mod4, wup, cw, wdown, g, b, seq, alpha):
    m, d = x1.shape
    dff = wdown.shape[0]
    nf = dff // TF_FFN
    tps = seq // TM_FFN
    row = lambda i, j: (i, 0)
    full = lambda i, j: (0, 0)
    return pl.pallas_call(
        functools.partial(_ffn_kernel, alpha, tps, nf),
        out_shape=jax.ShapeDtypeStruct((m, d), F32),
        grid=(m // TM_FFN, nf),
        in_specs=[pl.BlockSpec((TM_FFN, d), row),
                  pl.BlockSpec((None, None, 1, d), lambda i, j: (i // tps, 4, 0, 0)),
                  pl.BlockSpec((None, None, 1, d), lambda i, j: (i // tps, 3, 0, 0)),
                  pl.BlockSpec((None, None, 1, d), lambda i, j: (i // tps, 5, 0, 0)),
                  pl.BlockSpec((d, TF_FFN), lambda i, j: (0, j)),
                  pl.BlockSpec((d, TF_FFN), lambda i, j: (0, nf + j)),
                  pl.BlockSpec((SUBLANES, TF_FFN), lambda i, j: (0, j)),
                  pl.BlockSpec((TF_FFN, d), lambda i, j: (j, 0)),
                  pl.BlockSpec((1, d), full),
                  pl.BlockSpec((1, d), full)],
        out_specs=pl.BlockSpec((TM_FFN, d), row),
        scratch_shapes=[pltpu.VMEM((TM_FFN, d), BF16),
                        pltpu.VMEM((TM_FFN, d), F32),
                        pltpu.VMEM((nf, SUBLANES, TF_FFN), F32)],
        compiler_params=_params(("arbitrary", "arbitrary")),
        name="ffn",
    )(x1, mod4, mod4, mod4, wup, wup, cw, wdown, g, b)


def _pad_rows(a, rows):
    return jnp.pad(a, ((0, rows - a.shape[0]), (0, 0)))


def kernel(x, c, w_cond, b_cond, w_in, conv_a, idx_kn_g, idx_kn_b, w_a, w_b, w_o, ln1_g, ln1_b,
           w_up, conv_f, w_down, ln2_g, ln2_b):
    batch, seq, d = x.shape
    depth = w_cond.shape[0]
    alpha = (2.0 * depth) ** 0.25
    d_attn = N_HEADS * HEAD_DIM
    d_kv = N_KV_HEADS * HEAD_DIM
    d_qi = IDX_HEADS * IDX_DIM
    d_conv = conv_a.shape[2]
    assert min(TOPK_MAX, seq // 4) == TOPK_MAX and seq % TM == 0 and TM % TK == 0 and TQ % (2 * TK) == 0
    o_q = 3 * d_conv
    o_k = o_q + d_attn
    o_v = o_k + d_kv
    o_qi = o_v + d_kv
    o_ki = o_qi + d_qi
    o_g = o_ki + IDX_DIM + IDX_HEADS
    cols = {"q": 0, "qi": d_attn, "k": d_attn + d_qi, "ga": d_attn + d_qi + d_kv,
            "gb": d_attn + d_qi + d_kv + d}
    q_scale = HEAD_DIM ** -0.5 * math.log2(math.e)
    colscale = jnp.concatenate([jnp.full((1, d_attn), q_scale, F32),
                                jnp.ones((1, d_qi + d_kv + 2 * d), F32)], axis=1)

    x2 = x.reshape(batch * seq, d)
    c_pad = _pad_rows(c, 2 * SUBLANES)
    for l in range(depth):
        mod = _mod_call(c_pad, w_cond[l], b_cond[l][None, :])[:batch]
        mod4 = mod.reshape(batch, N_MOD, 1, d)

        wt = jnp.swapaxes(w_in[l], 0, 1)
        groups = ((0, o_q), (o_q, o_k), (o_qi, o_ki), (o_k, o_v), (o_g, o_g + 2 * d), (o_v, o_qi))
        src_rows = tuple(r for lo, hi in groups for r in range(lo, hi, REPACK_ROWS))
        wp = _repack_call(wt, src_rows)
        row_main, n_main, row_v = o_q, d_attn + d_qi + d_kv + 2 * d, o_q + d_attn + d_qi + d_kv + 2 * d
        wk = jnp.pad(wt[o_ki:o_g], ((0, LANES - (o_g - o_ki)), (0, 0))).astype(BF16)
        kg = jnp.pad(idx_kn_g[l], (0, LANES - IDX_DIM))[None, :]
        kb = jnp.pad(idx_kn_b[l], (0, LANES - IDX_DIM))[None, :]

        y_a = _conv_a_call(x2, mod4, wp, d_conv, _pad_rows(conv_a[l], SUBLANES), seq)
        p2, ka, kbo, wit, vt = _proj_call(x2, mod4, wp, row_main, n_main, row_v, d_kv, colscale, wk, kg, kb, seq)
        y_b = _attn_call(p2, ka, kbo, wit, vt, batch, seq, cols)
        x2 = _mix_call(y_a, y_b, w_a[l].astype(BF16), w_b[l].astype(BF16), p2, cols, w_o[l].astype(BF16),
                       x2, mod4, ln1_g[l][None, :], ln1_b[l][None, :], seq, alpha)
        x2 = _ffn_call(x2, mod4, w_up[l].astype(BF16), _pad_rows(conv_f[l], SUBLANES),
                       w_down[l].astype(BF16), ln2_g[l][None, :], ln2_b[l][None, :], seq, alpha)
    return x2.reshape(batch, seq, d)
```

```python
import functools
import math

import jax
import jax.numpy as jnp
from jax import lax
from jax.experimental import pallas as pl
from jax.experimental.pallas import tpu as pltpu

F32 = jnp.float32
BF16 = jnp.bfloat16
I32 = jnp.int32

CHUNK = 64
N_HEADS = 16
N_KV_HEADS = 4
HEAD_DIM = 128
GROUP = N_HEADS // N_KV_HEADS
IDX_HEADS = 16
IDX_DIM = 64
IDX_W_SCALE = (IDX_HEADS * IDX_DIM) ** -0.5
TOPK_MAX = 256
LN_EPS = 1e-5
N_MOD = 6

LANES = 128
SUBLANES = 8
VMEM_LIMIT = 56 * 1024 * 1024

TM = 1024
TN_CONV = 512
CONV_HALF = 256
TN_PROJ = 1536
TM_MIX = 512
TN_MIX = 512
MIX_HALF = 256
REPACK_ROWS = 512
TM_FFN = 512
TF_FFN = 512
FFN_HALF = 256
TQ = 512
TK = 256
ONES_ROWS = 16
V_ROWS = HEAD_DIM + ONES_ROWS

NEG_BIAS = -1e30
M_INIT = -3e38
INT_MIN = -(2 ** 31)
KEY_NEG_INF = INT_MIN + 0x7FFFFF
KEY_LOWEST_FINITE = KEY_NEG_INF + 1

_NT = (((1,), (1,)), ((), ()))


def _params(sem):
    return pltpu.CompilerParams(dimension_semantics=sem, vmem_limit_bytes=VMEM_LIMIT)


def _sigmoid(x):
    return 1.0 / (1.0 + jnp.exp(-x))


def _layer_norm_rows(r, g, b):
    mu = jnp.mean(r, axis=-1, keepdims=True)
    d = r - mu
    var = jnp.mean(d * d, axis=-1, keepdims=True)
    return d * lax.rsqrt(var + LN_EPS) * g + b


def _repack_kernel(src_ref, w_ref, o_ref):
    del src_ref
    o_ref[...] = w_ref[...].astype(BF16)


def _repack_call(wt, src_rows):
    d = wt.shape[1]
    nb = len(src_rows)
    return pl.pallas_call(
        _repack_kernel,
        out_shape=jax.ShapeDtypeStruct((nb * REPACK_ROWS, d), BF16),
        grid_spec=pltpu.PrefetchScalarGridSpec(
            num_scalar_prefetch=1,
            grid=(nb,),
            in_specs=[pl.BlockSpec((pl.Element(REPACK_ROWS), pl.Element(d)),
                                   lambda b, src: (pl.multiple_of(src[b], SUBLANES), 0))],
            out_specs=pl.BlockSpec((REPACK_ROWS, d), lambda b, src: (b, 0))),
        compiler_params=_params(("arbitrary",)),
        name="repack",
    )(jnp.asarray(src_rows, I32), wt)


def _mod_kernel(c_ref, w_ref, b_ref, o_ref):
    c = c_ref[...]
    ca = (c * _sigmoid(c)).astype(BF16)
    o_ref[...] = jnp.dot(ca, w_ref[...].astype(BF16), preferred_element_type=F32) + b_ref[...]


def _mod_call(c_pad, w_cond, b_cond):
    rows, d = c_pad.shape
    n = w_cond.shape[1]
    tn = 1024
    return pl.pallas_call(
        _mod_kernel,
        out_shape=jax.ShapeDtypeStruct((rows, n), F32),
        grid=(n // tn,),
        in_specs=[pl.BlockSpec((rows, d), lambda j: (0, 0)),
                  pl.BlockSpec((d, tn), lambda j: (0, j)),
                  pl.BlockSpec((1, tn), lambda j: (0, j))],
        out_specs=pl.BlockSpec((rows, tn), lambda j: (0, j)),
        compiler_params=_params(("arbitrary",)),
        name="mod",
    )(c_pad, w_cond, b_cond)


def _causal_conv_rows(p, first, halo_ref, j, cw_ref):
    tm = p.shape[0]
    halo = jnp.where(first, 0.0, halo_ref[j])
    halo_ref[j] = p[tm - SUBLANES:, :]
    ext = jnp.concatenate([halo, p], axis=0)
    prev1 = pltpu.roll(ext, 1, axis=0)[SUBLANES:, :]
    prev2 = pltpu.roll(ext, 2, axis=0)[SUBLANES:, :]
    return cw_ref[0:1, :] * prev2 + cw_ref[1:2, :] * prev1 + cw_ref[2:3, :] * p


def _conv_a_kernel(tiles_per_seq, x_ref, sc_ref, sh_ref, wb_ref, wc_ref, wh_ref, cw_ref,
                   o_ref, u_ref, halo_ref):
    i = pl.program_id(0)
    j = pl.program_id(1)

    @pl.when(j == 0)
    def _():
        u_ref[...] = (x_ref[...] * (1.0 + sc_ref[...]) + sh_ref[...]).astype(BF16)

    @pl.when((i == 0) & (j == 0))
    def _():
        halo_ref[...] = jnp.zeros(halo_ref.shape, F32)

    u = u_ref[...]
    first = (i % tiles_per_seq) == 0
    halves = [slice(h * CONV_HALF, (h + 1) * CONV_HALF) for h in range(TN_CONV // CONV_HALF)]
    proj = [tuple(lax.dot_general(u, w_ref[cols, :], _NT, preferred_element_type=F32)
                  for w_ref in (wb_ref, wc_ref, wh_ref)) for cols in halves]
    for cols, (cb, cc, ch) in zip(halves, proj):
        conv = _causal_conv_rows(cc * ch, first, halo_ref.at[:, :, cols], j, cw_ref.at[:, cols])
        o_ref[:, cols] = (cb * conv).astype(BF16)


def _conv_a_call(x2, mod4, wp, dc, cw, seq):
    m, d = x2.shape
    nj = dc // TN_CONV
    tps = seq // TM
    row = lambda i, j: (i, 0)
    return pl.pallas_call(
        functools.partial(_conv_a_kernel, tps),
        out_shape=jax.ShapeDtypeStruct((m, dc), BF16),
        grid=(m // TM, nj),
        in_specs=[pl.BlockSpec((TM, d), row),
                  pl.BlockSpec((None, None, 1, d), lambda i, j: (i // tps, 1, 0, 0)),
                  pl.BlockSpec((None, None, 1, d), lambda i, j: (i // tps, 0, 0, 0)),
                  pl.BlockSpec((TN_CONV, d), lambda i, j: (j, 0)),
                  pl.BlockSpec((TN_CONV, d), lambda i, j: (nj + j, 0)),
                  pl.BlockSpec((TN_CONV, d), lambda i, j: (2 * nj + j, 0)),
                  pl.BlockSpec((SUBLANES, TN_CONV), lambda i, j: (0, j))],
        out_specs=pl.BlockSpec((TM, TN_CONV), lambda i, j: (i, j)),
        scratch_shapes=[pltpu.VMEM((TM, d), BF16),
                        pltpu.VMEM((nj, SUBLANES, TN_CONV), F32)],
        compiler_params=_params(("arbitrary", "arbitrary")),
        name="conv_a",
    )(x2, mod4, mod4, wp, wp, wp, cw)


def _proj_kernel(x_ref, sc_ref, sh_ref, w_ref, cs_ref, wk_ref, kg_ref, kb_ref, wvt_ref,
                 o_ref, ka_ref, kbo_ref, wit_ref, vt_ref, u_ref):
    j = pl.program_id(1)

    @pl.when(j == 0)
    def _():
        ub = (x_ref[...] * (1.0 + sc_ref[...]) + sh_ref[...]).astype(BF16)
        u_ref[...] = ub
        kw = lax.dot_general(ub, wk_ref[...], _NT, preferred_element_type=F32)
        lane = lax.broadcasted_iota(I32, (1, LANES), 1)
        is_key = lane < IDX_DIM
        mu = jnp.sum(jnp.where(is_key, kw, 0.0), axis=-1, keepdims=True) * (1.0 / IDX_DIM)
        dk = jnp.where(is_key, kw - mu, 0.0)
        var = jnp.sum(dk * dk, axis=-1, keepdims=True) * (1.0 / IDX_DIM)
        kn = dk * lax.rsqrt(var + LN_EPS) * kg_ref[...] + kb_ref[...]
        ka_ref[...] = kn.astype(BF16)
        kbo_ref[...] = pltpu.roll(kn, IDX_DIM, axis=1).astype(BF16)
        wit_ref[...] = (kw * IDX_W_SCALE).T
        vt = lax.dot_general(wvt_ref[...], ub, _NT, preferred_element_type=F32).astype(BF16)
        ones = jnp.ones((ONES_ROWS, TK), BF16)
        for t in range(vt_ref.shape[0]):
            for n in range(N_KV_HEADS):
                vt_ref[t, n * V_ROWS:n * V_ROWS + HEAD_DIM, :] = (
                    vt[n * HEAD_DIM:(n + 1) * HEAD_DIM, t * TK:(t + 1) * TK])
                vt_ref[t, n * V_ROWS + HEAD_DIM:(n + 1) * V_ROWS, :] = ones

    acc = lax.dot_general(u_ref[...], w_ref[...], _NT, preferred_element_type=F32)
    o_ref[...] = (acc * cs_ref[...]).astype(BF16)


def _proj_call(x2, mod4, wp, row_main, n, row_v, d_kv, colscale, wk, kg, kb, seq):
    m, d = x2.shape
    main_blk = row_main // TN_PROJ
    v_blk = row_v // d_kv
    tps = seq // TM
    small = lambda i, j: (i, 0)
    full = lambda i, j: (0, 0)
    return pl.pallas_call(
        _proj_kernel,
        out_shape=(jax.ShapeDtypeStruct((m, n), BF16),
                   jax.ShapeDtypeStruct((m, LANES), BF16),
                   jax.ShapeDtypeStruct((m, LANES), BF16),
                   jax.ShapeDtypeStruct((LANES, m), F32),
                   jax.ShapeDtypeStruct((m // TK, N_KV_HEADS * V_ROWS, TK), BF16)),
        grid=(m // TM, n // TN_PROJ),
        in_specs=[pl.BlockSpec((TM, d), small),
                  pl.BlockSpec((None, None, 1, d), lambda i, j: (i // tps, 1, 0, 0)),
                  pl.BlockSpec((None, None, 1, d), lambda i, j: (i // tps, 0, 0, 0)),
                  pl.BlockSpec((TN_PROJ, d), lambda i, j: (main_blk + j, 0)),
                  pl.BlockSpec((1, TN_PROJ), lambda i, j: (0, j)),
                  pl.BlockSpec((LANES, d), full),
                  pl.BlockSpec((1, LANES), full),
                  pl.BlockSpec((1, LANES), full),
                  pl.BlockSpec((d_kv, d), lambda i, j: (v_blk, 0))],
        out_specs=(pl.BlockSpec((TM, TN_PROJ), lambda i, j: (i, j)),
                   pl.BlockSpec((TM, LANES), small),
                   pl.BlockSpec((TM, LANES), small),
                   pl.BlockSpec((LANES, TM), lambda i, j: (0, i)),
                   pl.BlockSpec((TM // TK, N_KV_HEADS * V_ROWS, TK), lambda i, j: (i, 0, 0))),
        scratch_shapes=[pltpu.VMEM((TM, d), BF16)],
        compiler_params=_params(("arbitrary", "arbitrary")),
        name="proj",
    )(x2, mod4, mod4, wp, colscale, wk, kg, kb, wp)


RED_ROWS = 32
CNT_ROWS = 16
POS_BITS = 16


def _attn_kernel(q_ref, k_ref, vt_ref, qi_ref, ka_ref, kb_ref, wit_ref, o_ref,
                 score_ref, sa_ref, sb_ref, ma_ref, mb_ref, p_ref, m_ref, acc_ref):
    i = pl.program_id(1)
    q0 = i * TQ
    n_kt = (q0 + TQ) // TK
    q_pos = q0 + lax.broadcasted_iota(I32, (1, TQ), 1)
    limit = (q_pos // CHUNK + 1) * CHUNK
    key_iota = lax.broadcasted_iota(I32, (TK, TQ), 0)

    def score_body(j, carry):
        ks = pl.multiple_of(j * TK, TK)
        ka = ka_ref[pl.ds(ks, TK), :]
        kb = kb_ref[pl.ds(ks, TK), :]
        acc = None
        for p in range(IDX_HEADS // 2):
            qp = qi_ref[:, p * LANES:(p + 1) * LANES]
            d0 = lax.dot_general(ka, qp, _NT, preferred_element_type=F32)
            d1 = lax.dot_general(kb, qp, _NT, preferred_element_type=F32)
            w0 = wit_ref[IDX_DIM + 2 * p:IDX_DIM + 2 * p + 1, :]
            w1 = wit_ref[IDX_DIM + 2 * p + 1:IDX_DIM + 2 * p + 2, :]
            t = w0 * jnp.maximum(d0, 0.0) + w1 * jnp.maximum(d1, 0.0)
            acc = t if acc is None else acc + t
        score_ref[j] = jnp.where(ks + key_iota < limit, acc, -jnp.inf)
        return carry

    lax.fori_loop(0, n_kt, score_body, 0)

    def key_to_f32(key):
        return pltpu.bitcast(key ^ ((key >> 31) & 0x7FFFFFFF), F32)

    def count_ge(key):
        cand = key_to_f32(key)

        def body(j, c):
            for r in range(TK // CNT_ROWS):
                c = c + jnp.where(score_ref[j, r * CNT_ROWS:(r + 1) * CNT_ROWS, :] >= cand, 1.0, 0.0)
            return c

        c = lax.fori_loop(0, n_kt, body, jnp.zeros((CNT_ROWS, TQ), F32))
        return jnp.sum(c, axis=0, keepdims=True)

    kf = float(TOPK_MAX)
    c0 = count_ge(jnp.zeros((1, TQ), I32))
    t0 = jnp.where(c0 >= kf, 0, INT_MIN).astype(I32)
    ct0 = jnp.where(c0 >= kf, c0, (n_kt * TK).astype(F32))

    def bit_body(b, carry):
        t, ct = carry
        cand = t | lax.shift_left(jnp.int32(1), 30 - b)
        c = count_ge(cand)
        keep = c >= kf
        return jnp.where(keep, cand, t), jnp.where(keep, c, ct)

    thr_key, cnt_thr = lax.fori_loop(0, 31, bit_body, (t0, ct0))
    tie = jnp.where(cnt_thr > kf, jnp.where(thr_key > KEY_NEG_INF, 1.0, 0.0), 0.0)
    any_tie = jnp.max(tie) > 0.0
    thr_key = jnp.maximum(thr_key, KEY_LOWEST_FINITE)
    thr = key_to_f32(thr_key)

    @pl.when(jnp.logical_not(any_tie))
    def _():
        def bias_body(j, carry):
            score_ref[j] = jnp.where(score_ref[j] >= thr, 0.0, NEG_BIAS)
            return carry

        lax.fori_loop(0, n_kt, bias_body, 0)

    @pl.when(any_tie)
    def _():
        keep_tied = kf - count_ge(thr_key + 1)

        def count_tied_below(pos):
            def body(j, c):
                idx = j * TK + key_iota
                hit = jnp.where(score_ref[j] == thr, jnp.where(idx < pos, 1.0, 0.0), 0.0)
                return c + jnp.sum(hit.reshape(TK // CNT_ROWS, CNT_ROWS, TQ), axis=0)

            c = lax.fori_loop(0, n_kt, body, jnp.zeros((CNT_ROWS, TQ), F32))
            return jnp.sum(c, axis=0, keepdims=True)

        def pos_body(b, pos):
            cand = pos | lax.shift_left(jnp.int32(1), POS_BITS - 1 - b)
            return jnp.where(count_tied_below(cand) <= keep_tied, cand, pos)

        pos = lax.fori_loop(0, POS_BITS, pos_body, jnp.zeros((1, TQ), I32))
        pos = jnp.where(tie > 0.0, pos, jnp.int32(2 ** POS_BITS))

        def bias_body(j, carry):
            sc = score_ref[j]
            idx = j * TK + key_iota
            tied = jnp.where(sc == thr, jnp.where(idx < pos, 0.0, NEG_BIAS), NEG_BIAS)
            score_ref[j] = jnp.where(sc > thr, 0.0, tied)
            return carry

        lax.fori_loop(0, n_kt, bias_body, 0)

    n_ch = TK // RED_ROWS
    for n in range(N_KV_HEADS):
        qg = [q_ref[:, (n * GROUP + g) * HEAD_DIM:(n * GROUP + g + 1) * HEAD_DIM] for g in range(GROUP)]
        m_ref[...] = jnp.full(m_ref.shape, M_INIT, F32)
        acc_ref[...] = jnp.zeros(acc_ref.shape, F32)

        def logits(j, dst_ref, max_ref, n=n, qg=qg):
            ks = pl.multiple_of(j * TK, TK)
            kt = k_ref[pl.ds(ks, TK), n * HEAD_DIM:(n + 1) * HEAD_DIM]
            bias = score_ref[j]
            for g in range(GROUP):
                s = lax.dot_general(kt, qg[g], _NT, preferred_element_type=F32) + bias
                dst_ref[g] = s
                part = jnp.max(s.reshape(n_ch, RED_ROWS, TQ), axis=0)
                max_ref[g] = jnp.max(part, axis=0, keepdims=True)

        def softmax_pv(j, src_ref, max_ref, slot, n=n):
            vt = vt_ref[j, n * V_ROWS:(n + 1) * V_ROWS, :]
            for g in range(GROUP):
                m_old = m_ref[g]
                m_new = jnp.maximum(m_old, max_ref[g])
                alpha = jnp.exp2(m_old - m_new)
                m_ref[g] = m_new
                for c in range(n_ch):
                    rows = slice(c * RED_ROWS, (c + 1) * RED_ROWS)
                    p_ref[slot, g, rows, :] = jnp.exp2((src_ref[g, rows, :] - m_new).astype(BF16))
                acc_ref[g] = alpha * acc_ref[g] + jnp.dot(vt, p_ref[slot, g], preferred_element_type=F32)

        def pair_body(jj, carry):
            j = 2 * jj
            logits(j, sa_ref, ma_ref)
            logits(j + 1, sb_ref, mb_ref)
            softmax_pv(j, sa_ref, ma_ref, 0)
            softmax_pv(j + 1, sb_ref, mb_ref, 1)
            return carry

        lax.fori_loop(0, n_kt // 2, pair_body, 0)
        for g in range(GROUP):
            h = n * GROUP + g
            o = acc_ref[g, 0:HEAD_DIM, :] / acc_ref[g, HEAD_DIM:HEAD_DIM + 1, :]
            o_ref[:, h * HEAD_DIM:(h + 1) * HEAD_DIM] = o.T.astype(BF16)


def _attn_call(p2, ka, kb, wit, vt, batch, seq, cols):
    d_attn = N_HEADS * HEAD_DIM
    d_kv = N_KV_HEADS * HEAD_DIM
    d_qi = IDX_HEADS * IDX_DIM
    nq = seq // TQ
    nkt = seq // TK
    once = pl.Buffered(1)
    return pl.pallas_call(
        _attn_kernel,
        out_shape=jax.ShapeDtypeStruct((batch * seq, d_attn), BF16),
        grid=(batch, nq),
        in_specs=[pl.BlockSpec((TQ, d_attn), lambda b, i: (b * nq + i, cols["q"] // d_attn)),
                  pl.BlockSpec((seq, d_kv), lambda b, i: (b, cols["k"] // d_kv), pipeline_mode=once),
                  pl.BlockSpec((nkt, N_KV_HEADS * V_ROWS, TK), lambda b, i: (b, 0, 0), pipeline_mode=once),
                  pl.BlockSpec((TQ, d_qi), lambda b, i: (b * nq + i, cols["qi"] // d_qi)),
                  pl.BlockSpec((seq, LANES), lambda b, i: (b, 0), pipeline_mode=once),
                  pl.BlockSpec((seq, LANES), lambda b, i: (b, 0), pipeline_mode=once),
                  pl.BlockSpec((LANES, TQ), lambda b, i: (0, b * nq + i))],
        out_specs=pl.BlockSpec((TQ, d_attn), lambda b, i: (b * nq + i, 0)),
        scratch_shapes=[pltpu.VMEM((nkt, TK, TQ), F32),
                        pltpu.VMEM((GROUP, TK, TQ), F32),
                        pltpu.VMEM((GROUP, TK, TQ), F32),
                        pltpu.VMEM((GROUP, 1, TQ), F32),
                        pltpu.VMEM((GROUP, 1, TQ), F32),
                        pltpu.VMEM((2, GROUP, TK, TQ), BF16),
                        pltpu.VMEM((GROUP, 1, TQ), F32),
                        pltpu.VMEM((GROUP, V_ROWS, TQ), F32)],
        compiler_params=_params(("arbitrary", "arbitrary")),
        name="attn",
    )(p2, p2, vt, p2, ka, kb, wit)


def _mix_kernel(alpha, nj, ya_ref, yb_ref, wa_ref, wb_ref, ga_ref, gb_ref, wo_ref, x_ref, gate_ref,
                g_ref, b_ref, o_ref, acc_ref):
    j = pl.program_id(1)

    @pl.when(j == 0)
    def _():
        acc_ref[...] = jnp.zeros(acc_ref.shape, F32)

    ya = ya_ref[...]
    yb = yb_ref[...]
    halves = [slice(h * MIX_HALF, (h + 1) * MIX_HALF) for h in range(TN_MIX // MIX_HALF)]
    br = [(jnp.dot(ya, wa_ref[:, cols], preferred_element_type=F32),
           jnp.dot(yb, wb_ref[:, cols], preferred_element_type=F32)) for cols in halves]
    contrib = None
    for cols, (a, b) in zip(halves, br):
        merged = (_sigmoid(ga_ref[:, cols].astype(F32)) * a
                  + _sigmoid(gb_ref[:, cols].astype(F32)) * b).astype(BF16)
        part = jnp.dot(merged, wo_ref[cols, :], preferred_element_type=F32)
        contrib = part if contrib is None else contrib + part
    acc_ref[...] += contrib

    @pl.when(j == nj - 1)
    def _():
        r = alpha * x_ref[...] + (1.0 + gate_ref[...]) * acc_ref[...]
        o_ref[...] = _layer_norm_rows(r, g_ref[...], b_ref[...])


def _mix_call(ya, yb, wa, wb, p2, cols, wo, x2, mod4, g, b, seq, alpha):
    m, d = x2.shape
    dc = ya.shape[1]
    db = yb.shape[1]
    n = wa.shape[1]
    nj = n // TN_MIX
    ga_blk = cols["ga"] // TN_MIX
    gb_blk = cols["gb"] // TN_MIX
    tps = seq // TM_MIX
    row = lambda i, j: (i, 0)
    full = lambda i, j: (0, 0)
    return pl.pallas_call(
        functools.partial(_mix_kernel, alpha, nj),
        out_shape=jax.ShapeDtypeStruct((m, d), F32),
        grid=(m // TM_MIX, nj),
        in_specs=[pl.BlockSpec((TM_MIX, dc), row),
                  pl.BlockSpec((TM_MIX, db), row),
                  pl.BlockSpec((dc, TN_MIX), lambda i, j: (0, j)),
                  pl.BlockSpec((db, TN_MIX), lambda i, j: (0, j)),
                  pl.BlockSpec((TM_MIX, TN_MIX), lambda i, j: (i, ga_blk + j)),
                  pl.BlockSpec((TM_MIX, TN_MIX), lambda i, j: (i, gb_blk + j)),
                  pl.BlockSpec((TN_MIX, d), lambda i, j: (j, 0)),
                  pl.BlockSpec((TM_MIX, d), row),
                  pl.BlockSpec((None, None, 1, d), lambda i, j: (i // tps, 2, 0, 0)),
                  pl.BlockSpec((1, d), full),
                  pl.BlockSpec((1, d), full)],
        out_specs=pl.BlockSpec((TM_MIX, d), row),
        scratch_shapes=[pltpu.VMEM((TM_MIX, d), F32)],
        compiler_params=_params(("arbitrary", "arbitrary")),
        name="mix",
    )(ya, yb, wa, wb, p2, p2, wo, x2, mod4, g, b)


def _gelu_tanh(x):
    return 0.5 * x * (1.0 + jnp.tanh(0.7978845608028654 * (x + 0.044715 * (x * x * x))))


def _ffn_kernel(alpha, tiles_per_seq, nf, x_ref, sc_ref, sh_ref, gate_ref, wa_ref, wb_ref, cw_ref,
                wd_ref, g_ref, b_ref, o_ref, u_ref, acc_ref, halo_ref):
    i = pl.program_id(0)
    j = pl.program_id(1)

    @pl.when(j == 0)
    def _():
        u_ref[...] = (x_ref[...] * (1.0 + sc_ref[...]) + sh_ref[...]).astype(BF16)
        acc_ref[...] = jnp.zeros(acc_ref.shape, F32)

    @pl.when((i == 0) & (j == 0))
    def _():
        halo_ref[...] = jnp.zeros(halo_ref.shape, F32)

    u = u_ref[...]
    first = (i % tiles_per_seq) == 0
    halves = [slice(h * FFN_HALF, (h + 1) * FFN_HALF) for h in range(TF_FFN // FFN_HALF)]
    up = [(jnp.dot(u, wa_ref[:, cols], preferred_element_type=F32),
           jnp.dot(u, wb_ref[:, cols], preferred_element_type=F32)) for cols in halves]
    contrib = None
    for cols, (ha, hb) in zip(halves, up):
        conv = _causal_conv_rows(ha, first, halo_ref.at[:, :, cols], j, cw_ref.at[:, cols])
        act = (_gelu_tanh(conv) * hb).astype(BF16)
        part = jnp.dot(act, wd_ref[cols, :], preferred_element_type=F32)
        contrib = part if contrib is None else contrib + part
    acc_ref[...] += contrib

    @pl.when(j == nf - 1)
    def _():
        r = alpha * x_ref[...] + (1.0 + gate_ref[...]) * acc_ref[...]
        o_ref[...] = _layer_norm_rows(r, g_ref[...], b_ref[...])


def _ffn_call(x1, mod4, wup, cw, wdown, g, b, seq, alpha):
    m, d = x1.shape
    dff = wdown.shape[0]
    nf = dff // TF_FFN
    tps = seq // TM_FFN
    row = lambda i, j: (i, 0)
    full = lambda i, j: (0, 0)
    return pl.pallas_call(
        functools.partial(_ffn_kernel, alpha, tps, nf),
        out_shape=jax.ShapeDtypeStruct((m, d), F32),
        grid=(m // TM_FFN, nf),
        in_specs=[pl.BlockSpec((TM_FFN, d), row),
                  pl.BlockSpec((None, None, 1, d), lambda i, j: (i // tps, 4, 0, 0)),
                  pl.BlockSpec((None, None, 1, d), lambda i, j: (i // tps, 3, 0, 0)),
                  pl.BlockSpec((None, None, 1, d), lambda i, j: (i // tps, 5, 0, 0)),
                  pl.BlockSpec((d, TF_FFN), lambda i, j: (0, j)),
                  pl.BlockSpec((d, TF_FFN), lambda i, j: (0, nf + j)),
                  pl.BlockSpec((SUBLANES, TF_FFN), lambda i, j: (0, j)),
                  pl.BlockSpec((TF_FFN, d), lambda i, j: (j, 0)),
                  pl.BlockSpec((1, d), full),
                  pl.BlockSpec((1, d), full)],
        out_specs=pl.BlockSpec((TM_FFN, d), row),
        scratch_shapes=[pltpu.VMEM((TM_FFN, d), BF16),
                        pltpu.VMEM((TM_FFN, d), F32),
                        pltpu.VMEM((nf, SUBLANES, TF_FFN), F32)],
        compiler_params=_params(("arbitrary", "arbitrary")),
        name="ffn",
    )(x1, mod4, mod4, mod4, wup, wup, cw, wdown, g, b)


def _pad_rows(a, rows):
    return jnp.pad(a, ((0, rows - a.shape[0]), (0, 0)))


def kernel(x, c, w_cond, b_cond, w_in, conv_a, idx_kn_g, idx_kn_b, w_a, w_b, w_o, ln1_g, ln1_b,
           w_up, conv_f, w_down, ln2_g, ln2_b):
    batch, seq, d = x.shape
    depth = w_cond.shape[0]
    alpha = (2.0 * depth) ** 0.25
    d_attn = N_HEADS * HEAD_DIM
    d_kv = N_KV_HEADS * HEAD_DIM
    d_qi = IDX_HEADS * IDX_DIM
    d_conv = conv_a.shape[2]
    assert min(TOPK_MAX, seq // 4) == TOPK_MAX and seq % TM == 0 and TM % TK == 0 and TQ % (2 * TK) == 0
    assert seq < 2 ** (POS_BITS - 1)
    o_q = 3 * d_conv
    o_k = o_q + d_attn
    o_v = o_k + d_kv
    o_qi = o_v + d_kv
    o_ki = o_qi + d_qi
    o_g = o_ki + IDX_DIM + IDX_HEADS
    cols = {"q": 0, "qi": d_attn, "k": d_attn + d_qi, "ga": d_attn + d_qi + d_kv,
            "gb": d_attn + d_qi + d_kv + d}
    q_scale = HEAD_DIM ** -0.5 * math.log2(math.e)
    colscale = jnp.concatenate([jnp.full((1, d_attn), q_scale, F32),
                                jnp.ones((1, d_qi + d_kv + 2 * d), F32)], axis=1)

    x2 = x.reshape(batch * seq, d)
    c_pad = _pad_rows(c, 2 * SUBLANES)
    for l in range(depth):
        mod = _mod_call(c_pad, w_cond[l], b_cond[l][None, :])[:batch]
        mod4 = mod.reshape(batch, N_MOD, 1, d)

        wt = jnp.swapaxes(w_in[l], 0, 1)
        groups = ((0, o_q), (o_q, o_k), (o_qi, o_ki), (o_k, o_v), (o_g, o_g + 2 * d), (o_v, o_qi))
        src_rows = tuple(r for lo, hi in groups for r in range(lo, hi, REPACK_ROWS))
        wp = _repack_call(wt, src_rows)
        row_main, n_main, row_v = o_q, d_attn + d_qi + d_kv + 2 * d, o_q + d_attn + d_qi + d_kv + 2 * d
        wk = jnp.pad(wt[o_ki:o_g], ((0, LANES - (o_g - o_ki)), (0, 0))).astype(BF16)
        kg = jnp.pad(idx_kn_g[l], (0, LANES - IDX_DIM))[None, :]
        kb = jnp.pad(idx_kn_b[l], (0, LANES - IDX_DIM))[None, :]

        y_a = _conv_a_call(x2, mod4, wp, d_conv, _pad_rows(conv_a[l], SUBLANES), seq)
        p2, ka, kbo, wit, vt = _proj_call(x2, mod4, wp, row_main, n_main, row_v, d_kv, colscale, wk, kg, kb, seq)
        y_b = _attn_call(p2, ka, kbo, wit, vt, batch, seq, cols)
        x2 = _mix_call(y_a, y_b, w_a[l].astype(BF16), w_b[l].astype(BF16), p2, cols, w_o[l].astype(BF16),
                       x2, mod4, ln1_g[l][None, :], ln1_b[l][None, :], seq, alpha)
        x2 = _ffn_call(x2, mod4, w_up[l].astype(BF16), _pad_rows(conv_f[l], SUBLANES),
                       w_down[l].astype(BF16), ln2_g[l][None, :], ln2_b[l][None, :], seq, alpha)
    return x2.reshape(batch, seq, d)
```

```python
import functools
import math

import jax
import jax.numpy as jnp
from jax import lax
from jax.experimental import pallas as pl
from jax.experimental.pallas import tpu as pltpu

F32 = jnp.float32
BF16 = jnp.bfloat16
I32 = jnp.int32

CHUNK = 64
N_HEADS = 16
N_KV_HEADS = 4
HEAD_DIM = 128
GROUP = N_HEADS // N_KV_HEADS
IDX_HEADS = 16
IDX_DIM = 64
IDX_W_SCALE = (IDX_HEADS * IDX_DIM) ** -0.5
TOPK_MAX = 256
LN_EPS = 1e-5
N_MOD = 6

LANES = 128
SUBLANES = 8
VMEM_LIMIT = 56 * 1024 * 1024

TM = 1024
TN_CONV = 512
CONV_HALF = 256
TN_PROJ = 1536
TM_MIX = 512
TN_MIX = 512
MIX_HALF = 256
REPACK_ROWS = 512
TM_FFN = 512
TF_FFN = 512
FFN_HALF = 256
TQ = 512
TK = 256
ONES_ROWS = 16
V_ROWS = HEAD_DIM + ONES_ROWS

NEG_BIAS = -1e30
M_INIT = -3e38
INT_MIN = -(2 ** 31)
KEY_NEG_INF = INT_MIN + 0x7FFFFF
KEY_LOWEST_FINITE = KEY_NEG_INF + 1

_NT = (((1,), (1,)), ((), ()))


def _params(sem):
    return pltpu.CompilerParams(dimension_semantics=sem, vmem_limit_bytes=VMEM_LIMIT)


def _sigmoid(x):
    return 1.0 / (1.0 + jnp.exp(-x))


def _layer_norm_rows(r, g, b):
    mu = jnp.mean(r, axis=-1, keepdims=True)
    d = r - mu
    var = jnp.mean(d * d, axis=-1, keepdims=True)
    return d * lax.rsqrt(var + LN_EPS) * g + b


def _repack_kernel(src_ref, w_ref, o_ref):
    del src_ref
    o_ref[...] = w_ref[...].astype(BF16)


def _repack_call(wt, src_rows):
    d = wt.shape[1]
    nb = len(src_rows)
    return pl.pallas_call(
        _repack_kernel,
        out_shape=jax.ShapeDtypeStruct((nb * REPACK_ROWS, d), BF16),
        grid_spec=pltpu.PrefetchScalarGridSpec(
            num_scalar_prefetch=1,
            grid=(nb,),
            in_specs=[pl.BlockSpec((pl.Element(REPACK_ROWS), pl.Element(d)),
                                   lambda b, src: (pl.multiple_of(src[b], SUBLANES), 0))],
            out_specs=pl.BlockSpec((REPACK_ROWS, d), lambda b, src: (b, 0))),
        compiler_params=_params(("arbitrary",)),
        name="repack",
    )(jnp.asarray(src_rows, I32), wt)


def _mod_kernel(c_ref, w_ref, b_ref, o_ref):
    c = c_ref[...]
    ca = (c * _sigmoid(c)).astype(BF16)
    o_ref[...] = jnp.dot(ca, w_ref[...].astype(BF16), preferred_element_type=F32) + b_ref[...]


def _mod_call(c_pad, w_cond, b_cond):
    rows, d = c_pad.shape
    n = w_cond.shape[1]
    tn = 1024
    return pl.pallas_call(
        _mod_kernel,
        out_shape=jax.ShapeDtypeStruct((rows, n), F32),
        grid=(n // tn,),
        in_specs=[pl.BlockSpec((rows, d), lambda j: (0, 0)),
                  pl.BlockSpec((d, tn), lambda j: (0, j)),
                  pl.BlockSpec((1, tn), lambda j: (0, j))],
        out_specs=pl.BlockSpec((rows, tn), lambda j: (0, j)),
        compiler_params=_params(("arbitrary",)),
        name="mod",
    )(c_pad, w_cond, b_cond)


def _causal_conv_rows(p, first, halo_ref, j, cw_ref):
    tm = p.shape[0]
    halo = jnp.where(first, 0.0, halo_ref[j])
    halo_ref[j] = p[tm - SUBLANES:, :]
    ext = jnp.concatenate([halo, p], axis=0)
    prev1 = pltpu.roll(ext, 1, axis=0)[SUBLANES:, :]
    prev2 = pltpu.roll(ext, 2, axis=0)[SUBLANES:, :]
    return cw_ref[0:1, :] * prev2 + cw_ref[1:2, :] * prev1 + cw_ref[2:3, :] * p


def _conv_a_kernel(tiles_per_seq, x_ref, sc_ref, sh_ref, wb_ref, wc_ref, wh_ref, cw_ref,
                   o_ref, u_ref, halo_ref):
    i = pl.program_id(0)
    j = pl.program_id(1)

    @pl.when(j == 0)
    def _():
        u_ref[...] = (x_ref[...] * (1.0 + sc_ref[...]) + sh_ref[...]).astype(BF16)

    @pl.when((i == 0) & (j == 0))
    def _():
        halo_ref[...] = jnp.zeros(halo_ref.shape, F32)

    u = u_ref[...]
    first = (i % tiles_per_seq) == 0
    halves = [slice(h * CONV_HALF, (h + 1) * CONV_HALF) for h in range(TN_CONV // CONV_HALF)]
    proj = [tuple(lax.dot_general(u, w_ref[cols, :], _NT, preferred_element_type=F32)
                  for w_ref in (wb_ref, wc_ref, wh_ref)) for cols in halves]
    for cols, (cb, cc, ch) in zip(halves, proj):
        conv = _causal_conv_rows(cc * ch, first, halo_ref.at[:, :, cols], j, cw_ref.at[:, cols])
        o_ref[:, cols] = (cb * conv).astype(BF16)


def _conv_a_call(x2, mod4, wp, dc, cw, seq):
    m, d = x2.shape
    nj = dc // TN_CONV
    tps = seq // TM
    row = lambda i, j: (i, 0)
    return pl.pallas_call(
        functools.partial(_conv_a_kernel, tps),
        out_shape=jax.ShapeDtypeStruct((m, dc), BF16),
        grid=(m // TM, nj),
        in_specs=[pl.BlockSpec((TM, d), row),
                  pl.BlockSpec((None, None, 1, d), lambda i, j: (i // tps, 1, 0, 0)),
                  pl.BlockSpec((None, None, 1, d), lambda i, j: (i // tps, 0, 0, 0)),
                  pl.BlockSpec((TN_CONV, d), lambda i, j: (j, 0)),
                  pl.BlockSpec((TN_CONV, d), lambda i, j: (nj + j, 0)),
                  pl.BlockSpec((TN_CONV, d), lambda i, j: (2 * nj + j, 0)),
                  pl.BlockSpec((SUBLANES, TN_CONV), lambda i, j: (0, j))],
        out_specs=pl.BlockSpec((TM, TN_CONV), lambda i, j: (i, j)),
        scratch_shapes=[pltpu.VMEM((TM, d), BF16),
                        pltpu.VMEM((nj, SUBLANES, TN_CONV), F32)],
        compiler_params=_params(("arbitrary", "arbitrary")),
        name="conv_a",
    )(x2, mod4, mod4, wp, wp, wp, cw)


def _proj_kernel(x_ref, sc_ref, sh_ref, w_ref, cs_ref, wk_ref, kg_ref, kb_ref, wvt_ref,
                 o_ref, ka_ref, kbo_ref, wit_ref, vt_ref, u_ref):
    j = pl.program_id(1)

    @pl.when(j == 0)
    def _():
        ub = (x_ref[...] * (1.0 + sc_ref[...]) + sh_ref[...]).astype(BF16)
        u_ref[...] = ub
        kw = lax.dot_general(ub, wk_ref[...], _NT, preferred_element_type=F32)
        lane = lax.broadcasted_iota(I32, (1, LANES), 1)
        is_key = lane < IDX_DIM
        mu = jnp.sum(jnp.where(is_key, kw, 0.0), axis=-1, keepdims=True) * (1.0 / IDX_DIM)
        dk = jnp.where(is_key, kw - mu, 0.0)
        var = jnp.sum(dk * dk, axis=-1, keepdims=True) * (1.0 / IDX_DIM)
        kn = dk * lax.rsqrt(var + LN_EPS) * kg_ref[...] + kb_ref[...]
        ka_ref[...] = kn.astype(BF16)
        kbo_ref[...] = pltpu.roll(kn, IDX_DIM, axis=1).astype(BF16)
        wit_ref[...] = (kw * IDX_W_SCALE).T
        vt = lax.dot_general(wvt_ref[...], ub, _NT, preferred_element_type=F32).astype(BF16)
        ones = jnp.ones((ONES_ROWS, TK), BF16)
        for t in range(vt_ref.shape[0]):
            for n in range(N_KV_HEADS):
                vt_ref[t, n * V_ROWS:n * V_ROWS + HEAD_DIM, :] = (
                    vt[n * HEAD_DIM:(n + 1) * HEAD_DIM, t * TK:(t + 1) * TK])
                vt_ref[t, n * V_ROWS + HEAD_DIM:(n + 1) * V_ROWS, :] = ones

    acc = lax.dot_general(u_ref[...], w_ref[...], _NT, preferred_element_type=F32)
    o_ref[...] = (acc * cs_ref[...]).astype(BF16)


def _proj_call(x2, mod4, wp, row_main, n, row_v, d_kv, colscale, wk, kg, kb, seq):
    m, d = x2.shape
    main_blk = row_main // TN_PROJ
    v_blk = row_v // d_kv
    tps = seq // TM
    small = lambda i, j: (i, 0)
    full = lambda i, j: (0, 0)
    return pl.pallas_call(
        _proj_kernel,
        out_shape=(jax.ShapeDtypeStruct((m, n), BF16),
                   jax.ShapeDtypeStruct((m, LANES), BF16),
                   jax.ShapeDtypeStruct((m, LANES), BF16),
                   jax.ShapeDtypeStruct((LANES, m), F32),
                   jax.ShapeDtypeStruct((m // TK, N_KV_HEADS * V_ROWS, TK), BF16)),
        grid=(m // TM, n // TN_PROJ),
        in_specs=[pl.BlockSpec((TM, d), small),
                  pl.BlockSpec((None, None, 1, d), lambda i, j: (i // tps, 1, 0, 0)),
                  pl.BlockSpec((None, None, 1, d), lambda i, j: (i // tps, 0, 0, 0)),
                  pl.BlockSpec((TN_PROJ, d), lambda i, j: (main_blk + j, 0)),
                  pl.BlockSpec((1, TN_PROJ), lambda i, j: (0, j)),
                  pl.BlockSpec((LANES, d), full),
                  pl.BlockSpec((1, LANES), full),
                  pl.BlockSpec((1, LANES), full),
                  pl.BlockSpec((d_kv, d), lambda i, j: (v_blk, 0))],
        out_specs=(pl.BlockSpec((TM, TN_PROJ), lambda i, j: (i, j)),
                   pl.BlockSpec((TM, LANES), small),
                   pl.BlockSpec((TM, LANES), small),
                   pl.BlockSpec((LANES, TM), lambda i, j: (0, i)),
                   pl.BlockSpec((TM // TK, N_KV_HEADS * V_ROWS, TK), lambda i, j: (i, 0, 0))),
        scratch_shapes=[pltpu.VMEM((TM, d), BF16)],
        compiler_params=_params(("arbitrary", "arbitrary")),
        name="proj",
    )(x2, mod4, mod4, wp, colscale, wk, kg, kb, wp)


RED_ROWS = 32
CNT_ROWS = 16
BITS_PER_CHECK = 4
SEARCH_GROUPS = -(-31 // BITS_PER_CHECK)
POS_BITS = 16


def _attn_kernel(q_ref, k_ref, vt_ref, qi_ref, ka_ref, kb_ref, wit_ref, o_ref,
                 score_ref, pos_ref, sa_ref, sb_ref, ma_ref, mb_ref, p_ref, m_ref, acc_ref):
    i = pl.program_id(1)
    q0 = i * TQ
    n_kt = (q0 + TQ) // TK
    q_pos = q0 + lax.broadcasted_iota(I32, (1, TQ), 1)
    limit = (q_pos // CHUNK + 1) * CHUNK
    key_iota = lax.broadcasted_iota(I32, (TK, TQ), 0)

    def score_body(j, carry):
        ks = pl.multiple_of(j * TK, TK)
        ka = ka_ref[pl.ds(ks, TK), :]
        kb = kb_ref[pl.ds(ks, TK), :]
        acc = None
        for p in range(IDX_HEADS // 2):
            qp = qi_ref[:, p * LANES:(p + 1) * LANES]
            d0 = lax.dot_general(ka, qp, _NT, preferred_element_type=F32)
            d1 = lax.dot_general(kb, qp, _NT, preferred_element_type=F32)
            w0 = wit_ref[IDX_DIM + 2 * p:IDX_DIM + 2 * p + 1, :]
            w1 = wit_ref[IDX_DIM + 2 * p + 1:IDX_DIM + 2 * p + 2, :]
            t = w0 * jnp.maximum(d0, 0.0) + w1 * jnp.maximum(d1, 0.0)
            acc = t if acc is None else acc + t
        score_ref[j] = jnp.where(ks + key_iota < limit, acc, -jnp.inf)
        return carry

    lax.fori_loop(0, n_kt, score_body, 0)

    def key_to_f32(key):
        return pltpu.bitcast(key ^ ((key >> 31) & 0x7FFFFFFF), F32)

    def count_ge(key):
        cand = key_to_f32(key)

        def body(j, c):
            for r in range(TK // CNT_ROWS):
                c = c + jnp.where(score_ref[j, r * CNT_ROWS:(r + 1) * CNT_ROWS, :] >= cand, 1.0, 0.0)
            return c

        c = lax.fori_loop(0, n_kt, body, jnp.zeros((CNT_ROWS, TQ), F32))
        return jnp.sum(c, axis=0, keepdims=True)

    kf = float(TOPK_MAX)
    c0 = count_ge(jnp.zeros((1, TQ), I32))
    t0 = jnp.where(c0 >= kf, 0, INT_MIN).astype(I32)
    ct0 = jnp.where(c0 >= kf, c0, (n_kt * TK).astype(F32))

    def search_cond(carry):
        g, _, ct = carry
        unsettled = jnp.where(ct == kf, 0.0, jnp.where(limit <= TOPK_MAX, 0.0, 1.0))
        return jnp.logical_and(g < SEARCH_GROUPS, jnp.max(unsettled) > 0.0)

    def search_body(carry):
        g, t, ct = carry
        for s in range(BITS_PER_CHECK):
            b = 30 - (g * BITS_PER_CHECK + s)
            bit = jnp.where(b >= 0, lax.shift_left(jnp.int32(1), jnp.maximum(b, 0)), 0)
            cand = t | bit
            c = count_ge(cand)
            keep = c >= kf
            t = jnp.where(keep, cand, t)
            ct = jnp.where(keep, c, ct)
        return g + 1, t, ct

    _, thr_key, cnt_thr = lax.while_loop(search_cond, search_body, (jnp.int32(0), t0, ct0))
    tie = jnp.where(cnt_thr > kf, jnp.where(thr_key > KEY_NEG_INF, 1.0, 0.0), 0.0)
    thr_key = jnp.maximum(thr_key, KEY_LOWEST_FINITE)
    thr = key_to_f32(thr_key)
    pos_ref[...] = jnp.full(pos_ref.shape, 2 ** POS_BITS, I32)

    @pl.when(jnp.max(tie) > 0.0)
    def _():
        keep_tied = kf - count_ge(thr_key + 1)

        def count_tied_below(pos):
            def body(j, c):
                idx = j * TK + key_iota
                hit = jnp.where(score_ref[j] == thr, jnp.where(idx < pos, 1.0, 0.0), 0.0)
                return c + jnp.sum(hit.reshape(TK // CNT_ROWS, CNT_ROWS, TQ), axis=0)

            c = lax.fori_loop(0, n_kt, body, jnp.zeros((CNT_ROWS, TQ), F32))
            return jnp.sum(c, axis=0, keepdims=True)

        def pos_body(b, pos):
            cand = pos | lax.shift_left(jnp.int32(1), POS_BITS - 1 - b)
            return jnp.where(count_tied_below(cand) <= keep_tied, cand, pos)

        pos = lax.fori_loop(0, POS_BITS, pos_body, jnp.zeros((1, TQ), I32))
        pos_ref[...] = jnp.where(tie > 0.0, pos, pos_ref[...])

    pos = pos_ref[...]

    def bias_body(j, carry):
        sc = score_ref[j]
        idx = j * TK + key_iota
        tied = jnp.where(sc == thr, jnp.where(idx < pos, 0.0, NEG_BIAS), NEG_BIAS)
        score_ref[j] = jnp.where(sc > thr, 0.0, tied)
        return carry

    lax.fori_loop(0, n_kt, bias_body, 0)

    n_ch = TK // RED_ROWS
    for n in range(N_KV_HEADS):
        qg = [q_ref[:, (n * GROUP + g) * HEAD_DIM:(n * GROUP + g + 1) * HEAD_DIM] for g in range(GROUP)]
        m_ref[...] = jnp.full(m_ref.shape, M_INIT, F32)
        acc_ref[...] = jnp.zeros(acc_ref.shape, F32)

        def logits(j, dst_ref, max_ref, n=n, qg=qg):
            ks = pl.multiple_of(j * TK, TK)
            kt = k_ref[pl.ds(ks, TK), n * HEAD_DIM:(n + 1) * HEAD_DIM]
            bias = score_ref[j]
            for g in range(GROUP):
                s = lax.dot_general(kt, qg[g], _NT, preferred_element_type=F32) + bias
                dst_ref[g] = s
                part = jnp.max(s.reshape(n_ch, RED_ROWS, TQ), axis=0)
                max_ref[g] = jnp.max(part, axis=0, keepdims=True)

        def softmax_pv(j, src_ref, max_ref, slot, n=n):
            vt = vt_ref[j, n * V_ROWS:(n + 1) * V_ROWS, :]
            for g in range(GROUP):
                m_old = m_ref[g]
                m_new = jnp.maximum(m_old, max_ref[g])
                alpha = jnp.exp2(m_old - m_new)
                m_ref[g] = m_new
                for c in range(n_ch):
                    rows = slice(c * RED_ROWS, (c + 1) * RED_ROWS)
                    p_ref[slot, g, rows, :] = jnp.exp2((src_ref[g, rows, :] - m_new).astype(BF16))
                acc_ref[g] = alpha * acc_ref[g] + jnp.dot(vt, p_ref[slot, g], preferred_element_type=F32)

        def pair_body(jj, carry):
            j = 2 * jj
            logits(j, sa_ref, ma_ref)
            logits(j + 1, sb_ref, mb_ref)
            softmax_pv(j, sa_ref, ma_ref, 0)
            softmax_pv(j + 1, sb_ref, mb_ref, 1)
            return carry

        lax.fori_loop(0, n_kt // 2, pair_body, 0)
        for g in range(GROUP):
            h = n * GROUP + g
            o = acc_ref[g, 0:HEAD_DIM, :] / acc_ref[g, HEAD_DIM:HEAD_DIM + 1, :]
            o_ref[:, h * HEAD_DIM:(h + 1) * HEAD_DIM] = o.T.astype(BF16)


def _attn_call(p2, ka, kb, wit, vt, batch, seq, cols):
    d_attn = N_HEADS * HEAD_DIM
    d_kv = N_KV_HEADS * HEAD_DIM
    d_qi = IDX_HEADS * IDX_DIM
    nq = seq // TQ
    nkt = seq // TK
    once = pl.Buffered(1)
    return pl.pallas_call(
        _attn_kernel,
        out_shape=jax.ShapeDtypeStruct((batch * seq, d_attn), BF16),
        grid=(batch, nq),
        in_specs=[pl.BlockSpec((TQ, d_attn), lambda b, i: (b * nq + i, cols["q"] // d_attn)),
                  pl.BlockSpec((seq, d_kv), lambda b, i: (b, cols["k"] // d_kv), pipeline_mode=once),
                  pl.BlockSpec((nkt, N_KV_HEADS * V_ROWS, TK), lambda b, i: (b, 0, 0), pipeline_mode=once),
                  pl.BlockSpec((TQ, d_qi), lambda b, i: (b * nq + i, cols["qi"] // d_qi)),
                  pl.BlockSpec((seq, LANES), lambda b, i: (b, 0), pipeline_mode=once),
                  pl.BlockSpec((seq, LANES), lambda b, i: (b, 0), pipeline_mode=once),
                  pl.BlockSpec((LANES, TQ), lambda b, i: (0, b * nq + i))],
        out_specs=pl.BlockSpec((TQ, d_attn), lambda b, i: (b * nq + i, 0)),
        scratch_shapes=[pltpu.VMEM((nkt, TK, TQ), F32),
                        pltpu.VMEM((1, TQ), I32),
                        pltpu.VMEM((GROUP, TK, TQ), F32),
                        pltpu.VMEM((GROUP, TK, TQ), F32),
                        pltpu.VMEM((GROUP, 1, TQ), F32),
                        pltpu.VMEM((GROUP, 1, TQ), F32),
                        pltpu.VMEM((2, GROUP, TK, TQ), BF16),
                        pltpu.VMEM((GROUP, 1, TQ), F32),
                        pltpu.VMEM((GROUP, V_ROWS, TQ), F32)],
        compiler_params=_params(("arbitrary", "arbitrary")),
        name="attn",
    )(p2, p2, vt, p2, ka, kb, wit)


def _mix_kernel(alpha, nj, ya_ref, yb_ref, wa_ref, wb_ref, ga_ref, gb_ref, wo_ref, x_ref, gate_ref,
                g_ref, b_ref, o_ref, acc_ref):
    j = pl.program_id(1)

    @pl.when(j == 0)
    def _():
        acc_ref[...] = jnp.zeros(acc_ref.shape, F32)

    ya = ya_ref[...]
    yb = yb_ref[...]
    halves = [slice(h * MIX_HALF, (h + 1) * MIX_HALF) for h in range(TN_MIX // MIX_HALF)]
    br = [(jnp.dot(ya, wa_ref[:, cols], preferred_element_type=F32),
           jnp.dot(yb, wb_ref[:, cols], preferred_element_type=F32)) for cols in halves]
    contrib = None
    for cols, (a, b) in zip(halves, br):
        merged = (_sigmoid(ga_ref[:, cols].astype(F32)) * a
                  + _sigmoid(gb_ref[:, cols].astype(F32)) * b).astype(BF16)
        part = jnp.dot(merged, wo_ref[cols, :], preferred_element_type=F32)
        contrib = part if contrib is None else contrib + part
    acc_ref[...] += contrib

    @pl.when(j == nj - 1)
    def _():
        r = alpha * x_ref[...] + (1.0 + gate_ref[...]) * acc_ref[...]
        o_ref[...] = _layer_norm_rows(r, g_ref[...], b_ref[...])


def _mix_call(ya, yb, wa, wb, p2, cols, wo, x2, mod4, g, b, seq, alpha):
    m, d = x2.shape
    dc = ya.shape[1]
    db = yb.shape[1]
    n = wa.shape[1]
    nj = n // TN_MIX
    ga_blk = cols["ga"] // TN_MIX
    gb_blk = cols["gb"] // TN_MIX
    tps = seq // TM_MIX
    row = lambda i, j: (i, 0)
    full = lambda i, j: (0, 0)
    return pl.pallas_call(
        functools.partial(_mix_kernel, alpha, nj),
        out_shape=jax.ShapeDtypeStruct((m, d), F32),
        grid=(m // TM_MIX, nj),
        in_specs=[pl.BlockSpec((TM_MIX, dc), row),
                  pl.BlockSpec((TM_MIX, db), row),
                  pl.BlockSpec((dc, TN_MIX), lambda i, j: (0, j)),
                  pl.BlockSpec((db, TN_MIX), lambda i, j: (0, j)),
                  pl.BlockSpec((TM_MIX, TN_MIX), lambda i, j: (i, ga_blk + j)),
                  pl.BlockSpec((TM_MIX, TN_MIX), lambda i, j: (i, gb_blk + j)),
                  pl.BlockSpec((TN_MIX, d), lambda i, j: (j, 0)),
                  pl.BlockSpec((TM_MIX, d), row),
                  pl.BlockSpec((None, None, 1, d), lambda i, j: (i // tps, 2, 0, 0)),
                  pl.BlockSpec((1, d), full),
                  pl.BlockSpec((1, d), full)],
        out_specs=pl.BlockSpec((TM_MIX, d), row),
        scratch_shapes=[pltpu.VMEM((TM_MIX, d), F32)],
        compiler_params=_params(("arbitrary", "arbitrary")),
        name="mix",
    )(ya, yb, wa, wb, p2, p2, wo, x2, mod4, g, b)


def _gelu_tanh(x):
    return 0.5 * x * (1.0 + jnp.tanh(0.7978845608028654 * (x + 0.044715 * (x * x * x))))


def _ffn_kernel(alpha, tiles_per_seq, nf, x_ref, sc_ref, sh_ref, gate_ref, wa_ref, wb_ref, cw_ref,
                wd_ref, g_ref, b_ref, o_ref, u_ref, acc_ref, halo_ref):
    i = pl.program_id(0)
    j = pl.program_id(1)

    @pl.when(j == 0)
    def _():
        u_ref[...] = (x_ref[...] * (1.0 + sc_ref[...]) + sh_ref[...]).astype(BF16)
        acc_ref[...] = jnp.zeros(acc_ref.shape, F32)

    @pl.when((i == 0) & (j == 0))
    def _():
        halo_ref[...] = jnp.zeros(halo_ref.shape, F32)

    u = u_ref[...]
    first = (i % tiles_per_seq) == 0
    halves = [slice(h * FFN_HALF, (h + 1) * FFN_HALF) for h in range(TF_FFN // FFN_HALF)]
    up = [(jnp.dot(u, wa_ref[:, cols], preferred_element_type=F32),
           jnp.dot(u, wb_ref[:, cols], preferred_element_type=F32)) for cols in halves]
    contrib = None
    for cols, (ha, hb) in zip(halves, up):
        conv = _causal_conv_rows(ha, first, halo_ref.at[:, :, cols], j, cw_ref.at[:, cols])
        act = (_gelu_tanh(conv) * hb).astype(BF16)
        part = jnp.dot(act, wd_ref[cols, :], preferred_element_type=F32)
        contrib = part if contrib is None else contrib + part
    acc_ref[...] += contrib

    @pl.when(j == nf - 1)
    def _():
        r = alpha * x_ref[...] + (1.0 + gate_ref[...]) * acc_ref[...]
        o_ref[...] = _layer_norm_rows(r, g_ref[...], b_ref[...])


def _ffn_call(x1, mod4, wup, cw, wdown, g, b, seq, alpha):
    m, d = x1.shape
    dff = wdown.shape[0]
    nf = dff // TF_FFN
    tps = seq // TM_FFN
    row = lambda i, j: (i, 0)
    full = lambda i, j: (0, 0)
    return pl.pallas_call(
        functools.partial(_ffn_kernel, alpha, tps, nf),
        out_shape=jax.ShapeDtypeStruct((m, d), F32),
        grid=(m // TM_FFN, nf),
        in_specs=[pl.BlockSpec((TM_FFN, d), row),
                  pl.BlockSpec((None, None, 1, d), lambda i, j: (i // tps, 4, 0, 0)),
                  pl.BlockSpec((None, None, 1, d), lambda i, j: (i // tps, 3, 0, 0)),
                  pl.BlockSpec((None, None, 1, d), lambda i, j: (i // tps, 5, 0, 0)),
                  pl.BlockSpec((d, TF_FFN), lambda i, j: (0, j)),
                  pl.BlockSpec((d, TF_FFN), lambda i, j: (0, nf + j)),
                  pl.BlockSpec((SUBLANES, TF_FFN), lambda i, j: (0, j)),
                  pl.BlockSpec((TF_FFN, d), lambda i, j: (j, 0)),
                  pl.BlockSpec((1, d), full),
                  pl.BlockSpec((1, d), full)],
        out_specs=pl.BlockSpec((TM_FFN, d), row),
        scratch_shapes=[pltpu.VMEM((TM_FFN, d), BF16),
                        pltpu.VMEM((TM_FFN, d), F32),
                        pltpu.VMEM((nf, SUBLANES, TF_FFN), F32)],
        compiler_params=_params(("arbitrary", "arbitrary")),
        name="ffn",
    )(x1, mod4, mod4, mod4, wup, wup, cw, wdown, g, b)


def _pad_rows(a, rows):
    return jnp.pad(a, ((0, rows - a.shape[0]), (0, 0)))


def kernel(x, c, w_cond, b_cond, w_in, conv_a, idx_kn_g, idx_kn_b, w_a, w_b, w_o, ln1_g, ln1_b,
           w_up, conv_f, w_down, ln2_g, ln2_b):
    batch, seq, d = x.shape
    depth = w_cond.shape[0]
    alpha = (2.0 * depth) ** 0.25
    d_attn = N_HEADS * HEAD_DIM
    d_kv = N_KV_HEADS * HEAD_DIM
    d_qi = IDX_HEADS * IDX_DIM
    d_conv = conv_a.shape[2]
    assert min(TOPK_MAX, seq // 4) == TOPK_MAX and seq % TM == 0 and TM % TK == 0 and TQ % (2 * TK) == 0
    assert seq < 2 ** (POS_BITS - 1)
    o_q = 3 * d_conv
    o_k = o_q + d_attn
    o_v = o_k + d_kv
    o_qi = o_v + d_kv
    o_ki = o_qi + d_qi
    o_g = o_ki + IDX_DIM + IDX_HEADS
    cols = {"q": 0, "qi": d_attn, "k": d_attn + d_qi, "ga": d_attn + d_qi + d_kv,
            "gb": d_attn + d_qi + d_kv + d}
    q_scale = HEAD_DIM ** -0.5 * math.log2(math.e)
    colscale = jnp.concatenate([jnp.full((1, d_attn), q_scale, F32),
                                jnp.ones((1, d_qi + d_kv + 2 * d), F32)], axis=1)

    x2 = x.reshape(batch * seq, d)
    c_pad = _pad_rows(c, 2 * SUBLANES)
    for l in range(depth):
        mod = _mod_call(c_pad, w_cond[l], b_cond[l][None, :])[:batch]
        mod4 = mod.reshape(batch, N_MOD, 1, d)

        wt = jnp.swapaxes(w_in[l], 0, 1)
        groups = ((0, o_q), (o_q, o_k), (o_qi, o_ki), (o_k, o_v), (o_g, o_g + 2 * d), (o_v, o_qi))
        src_rows = tuple(r for lo, hi in groups for r in range(lo, hi, REPACK_ROWS))
        wp = _repack_call(wt, src_rows)
        row_main, n_main, row_v = o_q, d_attn + d_qi + d_kv + 2 * d, o_q + d_attn + d_qi + d_kv + 2 * d
        wk = jnp.pad(wt[o_ki:o_g], ((0, LANES - (o_g - o_ki)), (0, 0))).astype(BF16)
        kg = jnp.pad(idx_kn_g[l], (0, LANES - IDX_DIM))[None, :]
        kb = jnp.pad(idx_kn_b[l], (0, LANES - IDX_DIM))[None, :]

        y_a = _conv_a_call(x2, mod4, wp, d_conv, _pad_rows(conv_a[l], SUBLANES), seq)
        p2, ka, kbo, wit, vt = _proj_call(x2, mod4, wp, row_main, n_main, row_v, d_kv, colscale, wk, kg, kb, seq)
        y_b = _attn_call(p2, ka, kbo, wit, vt, batch, seq, cols)
        x2 = _mix_call(y_a, y_b, w_a[l].astype(BF16), w_b[l].astype(BF16), p2, cols, w_o[l].astype(BF16),
                       x2, mod4, ln1_g[l][None, :], ln1_b[l][None, :], seq, alpha)
        x2 = _ffn_call(x2, mod4, w_up[l].astype(BF16), _pad_rows(conv_f[l], SUBLANES),
                       w_down[l].astype(BF16), ln2_g[l][None, :], ln2_b[l][None, :], seq, alpha)
    return x2.reshape(batch, seq, d)
```

```python
import functools
import math

import jax
import jax.numpy as jnp
from jax import lax
from jax.experimental import pallas as pl
from jax.experimental.pallas import tpu as pltpu

F32 = jnp.float32
BF16 = jnp.bfloat16
I32 = jnp.int32

CHUNK = 64
N_HEADS = 16
N_KV_HEADS = 4
HEAD_DIM = 128
GROUP = N_HEADS // N_KV_HEADS
IDX_HEADS = 16
IDX_DIM = 64
IDX_W_SCALE = (IDX_HEADS * IDX_DIM) ** -0.5
TOPK_MAX = 256
LN_EPS = 1e-5
N_MOD = 6

LANES = 128
SUBLANES = 8
VMEM_LIMIT = 56 * 1024 * 1024

TM = 1024
TN_CONV = 512
CONV_HALF = 256
TN_PROJ = 1536
TM_MIX = 512
TN_MIX = 512
MIX_HALF = 256
REPACK_ROWS = 512
TM_FFN = 512
TF_FFN = 512
FFN_HALF = 256
TQ = 512
TK = 256
ONES_ROWS = 16
V_ROWS = HEAD_DIM + ONES_ROWS

NEG_BIAS = -1e30
M_INIT = -3e38
INT_MIN = -(2 ** 31)
KEY_NEG_INF = INT_MIN + 0x7FFFFF
KEY_LOWEST_FINITE = KEY_NEG_INF + 1

_NT = (((1,), (1,)), ((), ()))


def _params(sem):
    return pltpu.CompilerParams(dimension_semantics=sem, vmem_limit_bytes=VMEM_LIMIT)


def _sigmoid(x):
    return 1.0 / (1.0 + jnp.exp(-x))


def _layer_norm_rows(r, g, b):
    mu = jnp.mean(r, axis=-1, keepdims=True)
    d = r - mu
    var = jnp.mean(d * d, axis=-1, keepdims=True)
    return d * lax.rsqrt(var + LN_EPS) * g + b


def _repack_kernel(src_ref, w_ref, o_ref):
    del src_ref
    o_ref[...] = w_ref[...].astype(BF16)


def _repack_call(wt, src_rows):
    d = wt.shape[1]
    nb = len(src_rows)
    return pl.pallas_call(
        _repack_kernel,
        out_shape=jax.ShapeDtypeStruct((nb * REPACK_ROWS, d), BF16),
        grid_spec=pltpu.PrefetchScalarGridSpec(
            num_scalar_prefetch=1,
            grid=(nb,),
            in_specs=[pl.BlockSpec((pl.Element(REPACK_ROWS), pl.Element(d)),
                                   lambda b, src: (pl.multiple_of(src[b], SUBLANES), 0))],
            out_specs=pl.BlockSpec((REPACK_ROWS, d), lambda b, src: (b, 0))),
        compiler_params=_params(("arbitrary",)),
        name="repack",
    )(jnp.asarray(src_rows, I32), wt)


def _mod_kernel(c_ref, w_ref, b_ref, o_ref):
    c = c_ref[...]
    ca = (c * _sigmoid(c)).astype(BF16)
    o_ref[...] = jnp.dot(ca, w_ref[...].astype(BF16), preferred_element_type=F32) + b_ref[...]


def _mod_call(c_pad, w_cond, b_cond):
    rows, d = c_pad.shape
    n = w_cond.shape[1]
    tn = 1024
    return pl.pallas_call(
        _mod_kernel,
        out_shape=jax.ShapeDtypeStruct((rows, n), F32),
        grid=(n // tn,),
        in_specs=[pl.BlockSpec((rows, d), lambda j: (0, 0)),
                  pl.BlockSpec((d, tn), lambda j: (0, j)),
                  pl.BlockSpec((1, tn), lambda j: (0, j))],
        out_specs=pl.BlockSpec((rows, tn), lambda j: (0, j)),
        compiler_params=_params(("arbitrary",)),
        name="mod",
    )(c_pad, w_cond, b_cond)


def _causal_conv_rows(p, first, halo_ref, j, cw_ref):
    tm = p.shape[0]
    halo = jnp.where(first, 0.0, halo_ref[j])
    halo_ref[j] = p[tm - SUBLANES:, :]
    ext = jnp.concatenate([halo, p], axis=0)
    prev1 = pltpu.roll(ext, 1, axis=0)[SUBLANES:, :]
    prev2 = pltpu.roll(ext, 2, axis=0)[SUBLANES:, :]
    return cw_ref[0:1, :] * prev2 + cw_ref[1:2, :] * prev1 + cw_ref[2:3, :] * p


def _conv_a_kernel(tiles_per_seq, x_ref, sc_ref, sh_ref, wb_ref, wc_ref, wh_ref, cw_ref,
                   o_ref, u_ref, halo_ref):
    i = pl.program_id(0)
    j = pl.program_id(1)

    @pl.when(j == 0)
    def _():
        u_ref[...] = (x_ref[...] * (1.0 + sc_ref[...]) + sh_ref[...]).astype(BF16)

    @pl.when((i == 0) & (j == 0))
    def _():
        halo_ref[...] = jnp.zeros(halo_ref.shape, F32)

    u = u_ref[...]
    first = (i % tiles_per_seq) == 0
    halves = [slice(h * CONV_HALF, (h + 1) * CONV_HALF) for h in range(TN_CONV // CONV_HALF)]
    proj = [tuple(lax.dot_general(u, w_ref[cols, :], _NT, preferred_element_type=F32)
                  for w_ref in (wb_ref, wc_ref, wh_ref)) for cols in halves]
    for cols, (cb, cc, ch) in zip(halves, proj):
        conv = _causal_conv_rows(cc * ch, first, halo_ref.at[:, :, cols], j, cw_ref.at[:, cols])
        o_ref[:, cols] = (cb * conv).astype(BF16)


def _conv_a_call(x2, mod4, wp, dc, cw, seq):
    m, d = x2.shape
    nj = dc // TN_CONV
    tps = seq // TM
    row = lambda i, j: (i, 0)
    return pl.pallas_call(
        functools.partial(_conv_a_kernel, tps),
        out_shape=jax.ShapeDtypeStruct((m, dc), BF16),
        grid=(m // TM, nj),
        in_specs=[pl.BlockSpec((TM, d), row),
                  pl.BlockSpec((None, None, 1, d), lambda i, j: (i // tps, 1, 0, 0)),
                  pl.BlockSpec((None, None, 1, d), lambda i, j: (i // tps, 0, 0, 0)),
                  pl.BlockSpec((TN_CONV, d), lambda i, j: (j, 0)),
                  pl.BlockSpec((TN_CONV, d), lambda i, j: (nj + j, 0)),
                  pl.BlockSpec((TN_CONV, d), lambda i, j: (2 * nj + j, 0)),
                  pl.BlockSpec((SUBLANES, TN_CONV), lambda i, j: (0, j))],
        out_specs=pl.BlockSpec((TM, TN_CONV), lambda i, j: (i, j)),
        scratch_shapes=[pltpu.VMEM((TM, d), BF16),
                        pltpu.VMEM((nj, SUBLANES, TN_CONV), F32)],
        compiler_params=_params(("arbitrary", "arbitrary")),
        name="conv_a",
    )(x2, mod4, mod4, wp, wp, wp, cw)


def _proj_kernel(x_ref, sc_ref, sh_ref, w_ref, cs_ref, wk_ref, kg_ref, kb_ref, wvt_ref,
                 o_ref, ka_ref, kbo_ref, wit_ref, vt_ref, u_ref):
    j = pl.program_id(1)

    @pl.when(j == 0)
    def _():
        ub = (x_ref[...] * (1.0 + sc_ref[...]) + sh_ref[...]).astype(BF16)
        u_ref[...] = ub
        kw = lax.dot_general(ub, wk_ref[...], _NT, preferred_element_type=F32)
        lane = lax.broadcasted_iota(I32, (1, LANES), 1)
        is_key = lane < IDX_DIM
        mu = jnp.sum(jnp.where(is_key, kw, 0.0), axis=-1, keepdims=True) * (1.0 / IDX_DIM)
        dk = jnp.where(is_key, kw - mu, 0.0)
        var = jnp.sum(dk * dk, axis=-1, keepdims=True) * (1.0 / IDX_DIM)
        kn = dk * lax.rsqrt(var + LN_EPS) * kg_ref[...] + kb_ref[...]
        ka_ref[...] = kn.astype(BF16)
        kbo_ref[...] = pltpu.roll(kn, IDX_DIM, axis=1).astype(BF16)
        wit_ref[...] = (kw * IDX_W_SCALE).T
        vt = lax.dot_general(wvt_ref[...], ub, _NT, preferred_element_type=F32).astype(BF16)
        ones = jnp.ones((ONES_ROWS, TK), BF16)
        for t in range(vt_ref.shape[0]):
            for n in range(N_KV_HEADS):
                vt_ref[t, n * V_ROWS:n * V_ROWS + HEAD_DIM, :] = (
                    vt[n * HEAD_DIM:(n + 1) * HEAD_DIM, t * TK:(t + 1) * TK])
                vt_ref[t, n * V_ROWS + HEAD_DIM:(n + 1) * V_ROWS, :] = ones

    acc = lax.dot_general(u_ref[...], w_ref[...], _NT, preferred_element_type=F32)
    o_ref[...] = (acc * cs_ref[...]).astype(BF16)


def _proj_call(x2, mod4, wp, row_main, n, row_v, d_kv, colscale, wk, kg, kb, seq):
    m, d = x2.shape
    main_blk = row_main // TN_PROJ
    v_blk = row_v // d_kv
    tps = seq // TM
    small = lambda i, j: (i, 0)
    full = lambda i, j: (0, 0)
    return pl.pallas_call(
        _proj_kernel,
        out_shape=(jax.ShapeDtypeStruct((m, n), BF16),
                   jax.ShapeDtypeStruct((m, LANES), BF16),
                   jax.ShapeDtypeStruct((m, LANES), BF16),
                   jax.ShapeDtypeStruct((LANES, m), F32),
                   jax.ShapeDtypeStruct((m // TK, N_KV_HEADS * V_ROWS, TK), BF16)),
        grid=(m // TM, n // TN_PROJ),
        in_specs=[pl.BlockSpec((TM, d), small),
                  pl.BlockSpec((None, None, 1, d), lambda i, j: (i // tps, 1, 0, 0)),
                  pl.BlockSpec((None, None, 1, d), lambda i, j: (i // tps, 0, 0, 0)),
                  pl.BlockSpec((TN_PROJ, d), lambda i, j: (main_blk + j, 0)),
                  pl.BlockSpec((1, TN_PROJ), lambda i, j: (0, j)),
                  pl.BlockSpec((LANES, d), full),
                  pl.BlockSpec((1, LANES), full),
                  pl.BlockSpec((1, LANES), full),
                  pl.BlockSpec((d_kv, d), lambda i, j: (v_blk, 0))],
        out_specs=(pl.BlockSpec((TM, TN_PROJ), lambda i, j: (i, j)),
                   pl.BlockSpec((TM, LANES), small),
                   pl.BlockSpec((TM, LANES), small),
                   pl.BlockSpec((LANES, TM), lambda i, j: (0, i)),
                   pl.BlockSpec((TM // TK, N_KV_HEADS * V_ROWS, TK), lambda i, j: (i, 0, 0))),
        scratch_shapes=[pltpu.VMEM((TM, d), BF16)],
        compiler_params=_params(("arbitrary", "arbitrary")),
        name="proj",
    )(x2, mod4, mod4, wp, colscale, wk, kg, kb, wp)


RED_ROWS = 32
CNT_ROWS = 16
BITS_PER_CHECK = 4
SEARCH_GROUPS = -(-31 // BITS_PER_CHECK)
POS_BITS = 16


def _attn_kernel(q_ref, k_ref, vt_ref, qi_ref, ka_ref, kb_ref, wit_ref, o_ref,
                 score_ref, pos_ref, sa_ref, sb_ref, ma_ref, mb_ref, p_ref, m_ref, acc_ref):
    i = pl.program_id(1)
    q0 = i * TQ
    n_kt = (q0 + TQ) // TK
    q_pos = q0 + lax.broadcasted_iota(I32, (1, TQ), 1)
    limit = (q_pos // CHUNK + 1) * CHUNK
    key_iota = lax.broadcasted_iota(I32, (TK, TQ), 0)

    def score_body(j, carry):
        ks = pl.multiple_of(j * TK, TK)
        ka = ka_ref[pl.ds(ks, TK), :]
        kb = kb_ref[pl.ds(ks, TK), :]
        acc = None
        for p in range(IDX_HEADS // 2):
            qp = qi_ref[:, p * LANES:(p + 1) * LANES]
            d0 = lax.dot_general(ka, qp, _NT, preferred_element_type=F32)
            d1 = lax.dot_general(kb, qp, _NT, preferred_element_type=F32)
            w0 = wit_ref[IDX_DIM + 2 * p:IDX_DIM + 2 * p + 1, :]
            w1 = wit_ref[IDX_DIM + 2 * p + 1:IDX_DIM + 2 * p + 2, :]
            t = w0 * jnp.maximum(d0, 0.0) + w1 * jnp.maximum(d1, 0.0)
            acc = t if acc is None else acc + t
        score_ref[j] = jnp.where(ks + key_iota < limit, acc, -jnp.inf)
        return carry

    lax.fori_loop(0, n_kt, score_body, 0)

    def key_to_f32(key):
        return pltpu.bitcast(key ^ ((key >> 31) & 0x7FFFFFFF), F32)

    def count_ge(key):
        cand = key_to_f32(key)

        def body(j, c):
            for r in range(TK // CNT_ROWS):
                c = c + jnp.where(score_ref[j, r * CNT_ROWS:(r + 1) * CNT_ROWS, :] >= cand, 1.0, 0.0)
            return c

        c = lax.fori_loop(0, n_kt, body, jnp.zeros((CNT_ROWS, TQ), F32))
        return jnp.sum(c, axis=0, keepdims=True)

    kf = float(TOPK_MAX)
    c0 = count_ge(jnp.zeros((1, TQ), I32))
    t0 = jnp.where(c0 >= kf, 0, INT_MIN).astype(I32)
    ct0 = jnp.where(c0 >= kf, c0, (n_kt * TK).astype(F32))

    def search_cond(carry):
        g, _, ct = carry
        unsettled = jnp.where(ct == kf, 0.0, jnp.where(limit <= TOPK_MAX, 0.0, 1.0))
        return jnp.logical_and(g < SEARCH_GROUPS, jnp.max(unsettled) > 0.0)

    def search_body(carry):
        g, t, ct = carry
        for s in range(BITS_PER_CHECK):
            b = 30 - (g * BITS_PER_CHECK + s)
            bit = jnp.where(b >= 0, lax.shift_left(jnp.int32(1), jnp.maximum(b, 0)), 0)
            cand = t | bit
            c = count_ge(cand)
            keep = c >= kf
            t = jnp.where(keep, cand, t)
            ct = jnp.where(keep, c, ct)
        return g + 1, t, ct

    _, thr_key, cnt_thr = lax.while_loop(search_cond, search_body, (jnp.int32(0), t0, ct0))
    tie = jnp.where(cnt_thr > kf, jnp.where(thr_key > KEY_NEG_INF, 1.0, 0.0), 0.0)
    thr_key = jnp.maximum(thr_key, KEY_LOWEST_FINITE)
    thr = key_to_f32(thr_key)
    pos_ref[...] = jnp.full(pos_ref.shape, 2 ** POS_BITS, I32)

    @pl.when(jnp.max(tie) > 0.0)
    def _():
        keep_tied = kf - count_ge(thr_key + 1)

        def count_tied_below(pos):
            def body(j, c):
                idx = j * TK + key_iota
                hit = jnp.where(score_ref[j] == thr, jnp.where(idx < pos, 1.0, 0.0), 0.0)
                return c + jnp.sum(hit.reshape(TK // CNT_ROWS, CNT_ROWS, TQ), axis=0)

            c = lax.fori_loop(0, n_kt, body, jnp.zeros((CNT_ROWS, TQ), F32))
            return jnp.sum(c, axis=0, keepdims=True)

        def pos_body(b, pos):
            cand = pos | lax.shift_left(jnp.int32(1), POS_BITS - 1 - b)
            return jnp.where(count_tied_below(cand) <= keep_tied, cand, pos)

        pos = lax.fori_loop(0, POS_BITS, pos_body, jnp.zeros((1, TQ), I32))
        pos_ref[...] = jnp.where(tie > 0.0, pos, pos_ref[...])

    pos = pos_ref[...]

    def bias_body(j, carry):
        sc = score_ref[j]
        idx = j * TK + key_iota
        tied = jnp.where(sc == thr, jnp.where(idx < pos, 0.0, NEG_BIAS), NEG_BIAS)
        score_ref[j] = jnp.where(sc > thr, 0.0, tied)
        return carry

    lax.fori_loop(0, n_kt, bias_body, 0)

    n_ch = TK // RED_ROWS
    for n in range(N_KV_HEADS):
        qg = [q_ref[:, (n * GROUP + g) * HEAD_DIM:(n * GROUP + g + 1) * HEAD_DIM] for g in range(GROUP)]
        m_ref[...] = jnp.full(m_ref.shape, M_INIT, F32)
        acc_ref[...] = jnp.zeros(acc_ref.shape, F32)

        def logits(j, dst_ref, max_ref, n=n, qg=qg):
            ks = pl.multiple_of(j * TK, TK)
            kt = k_ref[pl.ds(ks, TK), n * HEAD_DIM:(n + 1) * HEAD_DIM]
            bias = score_ref[j]
            for g in range(GROUP):
                s = lax.dot_general(kt, qg[g], _NT, preferred_element_type=F32) + bias
                dst_ref[g] = s
                part = jnp.max(s.reshape(n_ch, RED_ROWS, TQ), axis=0)
                max_ref[g] = jnp.max(part, axis=0, keepdims=True)

        def softmax_pv(j, src_ref, max_ref, slot, n=n):
            vt = vt_ref[j, n * V_ROWS:(n + 1) * V_ROWS, :]
            for g in range(GROUP):
                m_old = m_ref[g]
                m_new = jnp.maximum(m_old, max_ref[g])
                alpha = jnp.exp2(m_old - m_new)
                m_ref[g] = m_new
                for c in range(n_ch):
                    rows = slice(c * RED_ROWS, (c + 1) * RED_ROWS)
                    p_ref[slot, g, rows, :] = jnp.exp2((src_ref[g, rows, :] - m_new).astype(BF16))
                acc_ref[g] = alpha * acc_ref[g] + jnp.dot(vt, p_ref[slot, g], preferred_element_type=F32)

        logits(0, sa_ref, ma_ref)

        def pair_body(jj, carry):
            j = 2 * jj
            logits(j + 1, sb_ref, mb_ref)
            softmax_pv(j, sa_ref, ma_ref, 0)
            logits(jnp.minimum(j + 2, n_kt - 1), sa_ref, ma_ref)
            softmax_pv(j + 1, sb_ref, mb_ref, 1)
            return carry

        lax.fori_loop(0, n_kt // 2, pair_body, 0)
        for g in range(GROUP):
            h = n * GROUP + g
            o = acc_ref[g, 0:HEAD_DIM, :] / acc_ref[g, HEAD_DIM:HEAD_DIM + 1, :]
            o_ref[:, h * HEAD_DIM:(h + 1) * HEAD_DIM] = o.T.astype(BF16)


def _attn_call(p2, ka, kb, wit, vt, batch, seq, cols):
    d_attn = N_HEADS * HEAD_DIM
    d_kv = N_KV_HEADS * HEAD_DIM
    d_qi = IDX_HEADS * IDX_DIM
    nq = seq // TQ
    nkt = seq // TK
    once = pl.Buffered(1)
    return pl.pallas_call(
        _attn_kernel,
        out_shape=jax.ShapeDtypeStruct((batch * seq, d_attn), BF16),
        grid=(batch, nq),
        in_specs=[pl.BlockSpec((TQ, d_attn), lambda b, i: (b * nq + i, cols["q"] // d_attn)),
                  pl.BlockSpec((seq, d_kv), lambda b, i: (b, cols["k"] // d_kv), pipeline_mode=once),
                  pl.BlockSpec((nkt, N_KV_HEADS * V_ROWS, TK), lambda b, i: (b, 0, 0), pipeline_mode=once),
                  pl.BlockSpec((TQ, d_qi), lambda b, i: (b * nq + i, cols["qi"] // d_qi)),
                  pl.BlockSpec((seq, LANES), lambda b, i: (b, 0), pipeline_mode=once),
                  pl.BlockSpec((seq, LANES), lambda b, i: (b, 0), pipeline_mode=once),
                  pl.BlockSpec((LANES, TQ), lambda b, i: (0, b * nq + i))],
        out_specs=pl.BlockSpec((TQ, d_attn), lambda b, i: (b * nq + i, 0)),
        scratch_shapes=[pltpu.VMEM((nkt, TK, TQ), F32),
                        pltpu.VMEM((1, TQ), I32),
                        pltpu.VMEM((GROUP, TK, TQ), F32),
                        pltpu.VMEM((GROUP, TK, TQ), F32),
                        pltpu.VMEM((GROUP, 1, TQ), F32),
                        pltpu.VMEM((GROUP, 1, TQ), F32),
                        pltpu.VMEM((2, GROUP, TK, TQ), BF16),
                        pltpu.VMEM((GROUP, 1, TQ), F32),
                        pltpu.VMEM((GROUP, V_ROWS, TQ), F32)],
        compiler_params=_params(("arbitrary", "arbitrary")),
        name="attn",
    )(p2, p2, vt, p2, ka, kb, wit)


def _mix_kernel(alpha, nj, ya_ref, yb_ref, wa_ref, wb_ref, ga_ref, gb_ref, wo_ref, x_ref, gate_ref,
                g_ref, b_ref, o_ref, acc_ref):
    j = pl.program_id(1)

    @pl.when(j == 0)
    def _():
        acc_ref[...] = jnp.zeros(acc_ref.shape, F32)

    ya = ya_ref[...]
    yb = yb_ref[...]
    halves = [slice(h * MIX_HALF, (h + 1) * MIX_HALF) for h in range(TN_MIX // MIX_HALF)]
    br = [(jnp.dot(ya, wa_ref[:, cols], preferred_element_type=F32),
           jnp.dot(yb, wb_ref[:, cols], preferred_element_type=F32)) for cols in halves]
    contrib = None
    for cols, (a, b) in zip(halves, br):
        merged = (_sigmoid(ga_ref[:, cols].astype(F32)) * a
                  + _sigmoid(gb_ref[:, cols].astype(F32)) * b).astype(BF16)
        part = jnp.dot(merged, wo_ref[cols, :], preferred_element_type=F32)
        contrib = part if contrib is None else contrib + part
    acc_ref[...] += contrib

    @pl.when(j == nj - 1)
    def _():
        r = alpha * x_ref[...] + (1.0 + gate_ref[...]) * acc_ref[...]
        o_ref[...] = _layer_norm_rows(r, g_ref[...], b_ref[...])


def _mix_call(ya, yb, wa, wb, p2, cols, wo, x2, mod4, g, b, seq, alpha):
    m, d = x2.shape
    dc = ya.shape[1]
    db = yb.shape[1]
    n = wa.shape[1]
    nj = n // TN_MIX
    ga_blk = cols["ga"] // TN_MIX
    gb_blk = cols["gb"] // TN_MIX
    tps = seq // TM_MIX
    row = lambda i, j: (i, 0)
    full = lambda i, j: (0, 0)
    return pl.pallas_call(
        functools.partial(_mix_kernel, alpha, nj),
        out_shape=jax.ShapeDtypeStruct((m, d), F32),
        grid=(m // TM_MIX, nj),
        in_specs=[pl.BlockSpec((TM_MIX, dc), row),
                  pl.BlockSpec((TM_MIX, db), row),
                  pl.BlockSpec((dc, TN_MIX), lambda i, j: (0, j)),
                  pl.BlockSpec((db, TN_MIX), lambda i, j: (0, j)),
                  pl.BlockSpec((TM_MIX, TN_MIX), lambda i, j: (i, ga_blk + j)),
                  pl.BlockSpec((TM_MIX, TN_MIX), lambda i, j: (i, gb_blk + j)),
                  pl.BlockSpec((TN_MIX, d), lambda i, j: (j, 0)),
                  pl.BlockSpec((TM_MIX, d), row),
                  pl.BlockSpec((None, None, 1, d), lambda i, j: (i // tps, 2, 0, 0)),
                  pl.BlockSpec((1, d), full),
                  pl.BlockSpec((1, d), full)],
        out_specs=pl.BlockSpec((TM_MIX, d), row),
        scratch_shapes=[pltpu.VMEM((TM_MIX, d), F32)],
        compiler_params=_params(("arbitrary", "arbitrary")),
        name="mix",
    )(ya, yb, wa, wb, p2, p2, wo, x2, mod4, g, b)


def _gelu_tanh(x):
    return 0.5 * x * (1.0 + jnp.tanh(0.7978845608028654 * (x + 0.044715 * (x * x * x))))


def _ffn_kernel(alpha, tiles_per_seq, nf, x_ref, sc_ref, sh_ref, gate_ref, wa_ref, wb_ref, cw_ref,
                wd_ref, g_ref, b_ref, o_ref, u_ref, acc_ref, halo_ref):
    i = pl.program_id(0)
    j = pl.program_id(1)

    @pl.when(j == 0)
    def _():
        u_ref[...] = (x_ref[...] * (1.0 + sc_ref[...]) + sh_ref[...]).astype(BF16)
        acc_ref[...] = jnp.zeros(acc_ref.shape, F32)

    @pl.when((i == 0) & (j == 0))
    def _():
        halo_ref[...] = jnp.zeros(halo_ref.shape, F32)

    u = u_ref[...]
    first = (i % tiles_per_seq) == 0
    halves = [slice(h * FFN_HALF, (h + 1) * FFN_HALF) for h in range(TF_FFN // FFN_HALF)]
    up = [(jnp.dot(u, wa_ref[:, cols], preferred_element_type=F32),
           jnp.dot(u, wb_ref[:, cols], preferred_element_type=F32)) for cols in halves]
    contrib = None
    for cols, (ha, hb) in zip(halves, up):
        conv = _causal_conv_rows(ha, first, halo_ref.at[:, :, cols], j, cw_ref.at[:, cols])
        act = (_gelu_tanh(conv) * hb).astype(BF16)
        part = jnp.dot(act, wd_ref[cols, :], preferred_element_type=F32)
        contrib = part if contrib is None else contrib + part
    acc_ref[...] += contrib

    @pl.when(j == nf - 1)
    def _():
        r = alpha * x_ref[...] + (1.0 + gate_ref[...]) * acc_ref[...]
        o_ref[...] = _layer_norm_rows(r, g_ref[...], b_ref[...])


def _ffn_call(x1, mod4, wup, cw, wdown, g, b, seq, alpha):
    m, d = x1.shape
    dff = wdown.shape[0]
    nf = dff // TF_FFN
    tps = seq // TM_FFN
    row = lambda i, j: (i, 0)
    full = lambda i, j: (0, 0)
    return pl.pallas_call(
        functools.partial(_ffn_kernel, alpha, tps, nf),
        out_shape=jax.ShapeDtypeStruct((m, d), F32),
        grid=(m // TM_FFN, nf),
        in_specs=[pl.BlockSpec((TM_FFN, d), row),
                  pl.BlockSpec((None, None, 1, d), lambda i, j: (i // tps, 4, 0, 0)),
                  pl.BlockSpec((None, None, 1, d), lambda i, j: (i // tps, 3, 0, 0)),
                  pl.BlockSpec((None, None, 1, d), lambda i, j: (i // tps, 5, 0, 0)),
                  pl.BlockSpec((d, TF_FFN), lambda i, j: (0, j)),
                  pl.BlockSpec((d, TF_FFN), lambda i, j: (0, nf + j)),
                  pl.BlockSpec((SUBLANES, TF_FFN), lambda i, j: (0, j)),
                  pl.BlockSpec((TF_FFN, d), lambda i, j: (j, 0)),
                  pl.BlockSpec((1, d), full),
                  pl.BlockSpec((1, d), full)],
        out_specs=pl.BlockSpec((TM_FFN, d), row),
        scratch_shapes=[pltpu.VMEM((TM_FFN, d), BF16),
                        pltpu.VMEM((TM_FFN, d), F32),
                        pltpu.VMEM((nf, SUBLANES, TF_FFN), F32)],
        compiler_params=_params(("arbitrary", "arbitrary")),
        name="ffn",
    )(x1, mod4, mod4, mod4, wup, wup, cw, wdown, g, b)


def _pad_rows(a, rows):
    return jnp.pad(a, ((0, rows - a.shape[0]), (0, 0)))


def kernel(x, c, w_cond, b_cond, w_in, conv_a, idx_kn_g, idx_kn_b, w_a, w_b, w_o, ln1_g, ln1_b,
           w_up, conv_f, w_down, ln2_g, ln2_b):
    batch, seq, d = x.shape
    depth = w_cond.shape[0]
    alpha = (2.0 * depth) ** 0.25
    d_attn = N_HEADS * HEAD_DIM
    d_kv = N_KV_HEADS * HEAD_DIM
    d_qi = IDX_HEADS * IDX_DIM
    d_conv = conv_a.shape[2]
    assert min(TOPK_MAX, seq // 4) == TOPK_MAX and seq % TM == 0 and TM % TK == 0 and TQ % (2 * TK) == 0
    assert seq < 2 ** (POS_BITS - 1)
    o_q = 3 * d_conv
    o_k = o_q + d_attn
    o_v = o_k + d_kv
    o_qi = o_v + d_kv
    o_ki = o_qi + d_qi
    o_g = o_ki + IDX_DIM + IDX_HEADS
    cols = {"q": 0, "qi": d_attn, "k": d_attn + d_qi, "ga": d_attn + d_qi + d_kv,
            "gb": d_attn + d_qi + d_kv + d}
    q_scale = HEAD_DIM ** -0.5 * math.log2(math.e)
    colscale = jnp.concatenate([jnp.full((1, d_attn), q_scale, F32),
                                jnp.ones((1, d_qi + d_kv + 2 * d), F32)], axis=1)

    x2 = x.reshape(batch * seq, d)
    c_pad = _pad_rows(c, 2 * SUBLANES)
    for l in range(depth):
        mod = _mod_call(c_pad, w_cond[l], b_cond[l][None, :])[:batch]
        mod4 = mod.reshape(batch, N_MOD, 1, d)

        wt = jnp.swapaxes(w_in[l], 0, 1)
        groups = ((0, o_q), (o_q, o_k), (o_qi, o_ki), (o_k, o_v), (o_g, o_g + 2 * d), (o_v, o_qi))
        src_rows = tuple(r for lo, hi in groups for r in range(lo, hi, REPACK_ROWS))
        wp = _repack_call(wt, src_rows)
        row_main, n_main, row_v = o_q, d_attn + d_qi + d_kv + 2 * d, o_q + d_attn + d_qi + d_kv + 2 * d
        wk = jnp.pad(wt[o_ki:o_g], ((0, LANES - (o_g - o_ki)), (0, 0))).astype(BF16)
        kg = jnp.pad(idx_kn_g[l], (0, LANES - IDX_DIM))[None, :]
        kb = jnp.pad(idx_kn_b[l], (0, LANES - IDX_DIM))[None, :]

        y_a = _conv_a_call(x2, mod4, wp, d_conv, _pad_rows(conv_a[l], SUBLANES), seq)
        p2, ka, kbo, wit, vt = _proj_call(x2, mod4, wp, row_main, n_main, row_v, d_kv, colscale, wk, kg, kb, seq)
        y_b = _attn_call(p2, ka, kbo, wit, vt, batch, seq, cols)
        x2 = _mix_call(y_a, y_b, w_a[l].astype(BF16), w_b[l].astype(BF16), p2, cols, w_o[l].astype(BF16),
                       x2, mod4, ln1_g[l][None, :], ln1_b[l][None, :], seq, alpha)
        x2 = _ffn_call(x2, mod4, w_up[l].astype(BF16), _pad_rows(conv_f[l], SUBLANES),
                       w_down[l].astype(BF16), ln2_g[l][None, :], ln2_b[l][None, :], seq, alpha)
    return x2.reshape(batch, seq, d)
```

```python
import functools
import math

import jax
import jax.numpy as jnp
from jax import lax
from jax.experimental import pallas as pl
from jax.experimental.pallas import tpu as pltpu

F32 = jnp.float32
BF16 = jnp.bfloat16
I32 = jnp.int32

CHUNK = 64
N_HEADS = 16
N_KV_HEADS = 4
HEAD_DIM = 128
GROUP = N_HEADS // N_KV_HEADS
IDX_HEADS = 16
IDX_DIM = 64
IDX_W_SCALE = (IDX_HEADS * IDX_DIM) ** -0.5
TOPK_MAX = 256
LN_EPS = 1e-5
N_MOD = 6

LANES = 128
SUBLANES = 8
VMEM_LIMIT = 56 * 1024 * 1024

TM = 1024
TN_CONV = 512
CONV_HALF = 256
TN_PROJ = 1536
TM_MIX = 512
TN_MIX = 512
MIX_HALF = 256
REPACK_ROWS = 512
TM_FFN = 512
TF_FFN = 512
FFN_HALF = 256
TQ = 512
TK = 256
ONES_ROWS = 16
V_ROWS = HEAD_DIM + ONES_ROWS

NEG_BIAS = -1e30
M_INIT = -3e38
INT_MIN = -(2 ** 31)
KEY_NEG_INF = INT_MIN + 0x7FFFFF
KEY_LOWEST_FINITE = KEY_NEG_INF + 1

_NT = (((1,), (1,)), ((), ()))


def _params(sem):
    return pltpu.CompilerParams(dimension_semantics=sem, vmem_limit_bytes=VMEM_LIMIT)


def _sigmoid(x):
    return 1.0 / (1.0 + jnp.exp(-x))


def _layer_norm_rows(r, g, b):
    mu = jnp.mean(r, axis=-1, keepdims=True)
    d = r - mu
    var = jnp.mean(d * d, axis=-1, keepdims=True)
    return d * lax.rsqrt(var + LN_EPS) * g + b


def _repack_kernel(src_ref, w_ref, o_ref):
    del src_ref
    o_ref[...] = w_ref[...].astype(BF16)


def _repack_call(wt, src_rows):
    d = wt.shape[1]
    nb = len(src_rows)
    return pl.pallas_call(
        _repack_kernel,
        out_shape=jax.ShapeDtypeStruct((nb * REPACK_ROWS, d), BF16),
        grid_spec=pltpu.PrefetchScalarGridSpec(
            num_scalar_prefetch=1,
            grid=(nb,),
            in_specs=[pl.BlockSpec((pl.Element(REPACK_ROWS), pl.Element(d)),
                                   lambda b, src: (pl.multiple_of(src[b], SUBLANES), 0))],
            out_specs=pl.BlockSpec((REPACK_ROWS, d), lambda b, src: (b, 0))),
        compiler_params=_params(("arbitrary",)),
        name="repack",
    )(jnp.asarray(src_rows, I32), wt)


def _mod_kernel(c_ref, w_ref, b_ref, o_ref):
    c = c_ref[...]
    ca = (c * _sigmoid(c)).astype(BF16)
    o_ref[...] = jnp.dot(ca, w_ref[...].astype(BF16), preferred_element_type=F32) + b_ref[...]


def _mod_call(c_pad, w_cond, b_cond):
    rows, d = c_pad.shape
    n = w_cond.shape[1]
    tn = 1024
    return pl.pallas_call(
        _mod_kernel,
        out_shape=jax.ShapeDtypeStruct((rows, n), F32),
        grid=(n // tn,),
        in_specs=[pl.BlockSpec((rows, d), lambda j: (0, 0)),
                  pl.BlockSpec((d, tn), lambda j: (0, j)),
                  pl.BlockSpec((1, tn), lambda j: (0, j))],
        out_specs=pl.BlockSpec((rows, tn), lambda j: (0, j)),
        compiler_params=_params(("arbitrary",)),
        name="mod",
    )(c_pad, w_cond, b_cond)


def _causal_conv_rows(p, first, halo_ref, j, cw_ref):
    tm = p.shape[0]
    halo = jnp.where(first, 0.0, halo_ref[j])
    halo_ref[j] = p[tm - SUBLANES:, :]
    ext = jnp.concatenate([halo, p], axis=0)
    prev1 = pltpu.roll(ext, 1, axis=0)[SUBLANES:, :]
    prev2 = pltpu.roll(ext, 2, axis=0)[SUBLANES:, :]
    return cw_ref[0:1, :] * prev2 + cw_ref[1:2, :] * prev1 + cw_ref[2:3, :] * p


def _conv_a_kernel(tiles_per_seq, n_cast, x_ref, sc_ref, sh_ref, wb_ref, wc_ref, wh_ref, cw_ref, *refs):
    cast_in, (o_ref, *cast_out), (u_ref, halo_ref) = refs[:n_cast], refs[n_cast:2 * n_cast + 1], refs[2 * n_cast + 1:]
    i = pl.program_id(0)
    j = pl.program_id(1)
    for src_ref, dst_ref in zip(cast_in, cast_out):
        dst_ref[...] = src_ref[...].astype(BF16)

    @pl.when(j == 0)
    def _():
        u_ref[...] = (x_ref[...] * (1.0 + sc_ref[...]) + sh_ref[...]).astype(BF16)

    @pl.when((i == 0) & (j == 0))
    def _():
        halo_ref[...] = jnp.zeros(halo_ref.shape, F32)

    u = u_ref[...]
    first = (i % tiles_per_seq) == 0
    halves = [slice(h * CONV_HALF, (h + 1) * CONV_HALF) for h in range(TN_CONV // CONV_HALF)]
    proj = [tuple(lax.dot_general(u, w_ref[cols, :], _NT, preferred_element_type=F32)
                  for w_ref in (wb_ref, wc_ref, wh_ref)) for cols in halves]
    for cols, (cb, cc, ch) in zip(halves, proj):
        conv = _causal_conv_rows(cc * ch, first, halo_ref.at[:, :, cols], j, cw_ref.at[:, cols])
        o_ref[:, cols] = (cb * conv).astype(BF16)


def _conv_a_call(x2, mod4, wp, dc, cw, seq, cast_ws):
    m, d = x2.shape
    nj = dc // TN_CONV
    tps = seq // TM
    steps = (m // TM) * nj
    row = lambda i, j: (i, 0)
    slab = lambda i, j: (i * nj + j, 0)
    cast_blocks = [(w.shape[0] // steps, w.shape[1]) for w in cast_ws]
    assert all(w.shape[0] % (steps * 2 * SUBLANES) == 0 for w in cast_ws)
    return pl.pallas_call(
        functools.partial(_conv_a_kernel, tps, len(cast_ws)),
        out_shape=(jax.ShapeDtypeStruct((m, dc), BF16),
                   *[jax.ShapeDtypeStruct(w.shape, BF16) for w in cast_ws]),
        grid=(m // TM, nj),
        in_specs=[pl.BlockSpec((TM, d), row),
                  pl.BlockSpec((None, None, 1, d), lambda i, j: (i // tps, 1, 0, 0)),
                  pl.BlockSpec((None, None, 1, d), lambda i, j: (i // tps, 0, 0, 0)),
                  pl.BlockSpec((TN_CONV, d), lambda i, j: (j, 0)),
                  pl.BlockSpec((TN_CONV, d), lambda i, j: (nj + j, 0)),
                  pl.BlockSpec((TN_CONV, d), lambda i, j: (2 * nj + j, 0)),
                  pl.BlockSpec((SUBLANES, TN_CONV), lambda i, j: (0, j)),
                  *[pl.BlockSpec(b, slab) for b in cast_blocks]],
        out_specs=(pl.BlockSpec((TM, TN_CONV), lambda i, j: (i, j)),
                   *[pl.BlockSpec(b, slab) for b in cast_blocks]),
        scratch_shapes=[pltpu.VMEM((TM, d), BF16),
                        pltpu.VMEM((nj, SUBLANES, TN_CONV), F32)],
        compiler_params=_params(("arbitrary", "arbitrary")),
        name="conv_a",
    )(x2, mod4, mod4, wp, wp, wp, cw, *cast_ws)


def _proj_kernel(x_ref, sc_ref, sh_ref, w_ref, cs_ref, wk_ref, kg_ref, kb_ref, wvt_ref,
                 o_ref, ka_ref, kbo_ref, wit_ref, vt_ref, u_ref):
    j = pl.program_id(1)

    @pl.when(j == 0)
    def _():
        ub = (x_ref[...] * (1.0 + sc_ref[...]) + sh_ref[...]).astype(BF16)
        u_ref[...] = ub
        kw = lax.dot_general(ub, wk_ref[...], _NT, preferred_element_type=F32)
        lane = lax.broadcasted_iota(I32, (1, LANES), 1)
        is_key = lane < IDX_DIM
        mu = jnp.sum(jnp.where(is_key, kw, 0.0), axis=-1, keepdims=True) * (1.0 / IDX_DIM)
        dk = jnp.where(is_key, kw - mu, 0.0)
        var = jnp.sum(dk * dk, axis=-1, keepdims=True) * (1.0 / IDX_DIM)
        kn = dk * lax.rsqrt(var + LN_EPS) * kg_ref[...] + kb_ref[...]
        ka_ref[...] = kn.astype(BF16)
        kbo_ref[...] = pltpu.roll(kn, IDX_DIM, axis=1).astype(BF16)
        wit_ref[...] = (kw * IDX_W_SCALE).T
        vt = lax.dot_general(wvt_ref[...], ub, _NT, preferred_element_type=F32).astype(BF16)
        ones = jnp.ones((ONES_ROWS, TK), BF16)
        for t in range(vt_ref.shape[0]):
            for n in range(N_KV_HEADS):
                vt_ref[t, n * V_ROWS:n * V_ROWS + HEAD_DIM, :] = (
                    vt[n * HEAD_DIM:(n + 1) * HEAD_DIM, t * TK:(t + 1) * TK])
                vt_ref[t, n * V_ROWS + HEAD_DIM:(n + 1) * V_ROWS, :] = ones

    acc = lax.dot_general(u_ref[...], w_ref[...], _NT, preferred_element_type=F32)
    o_ref[...] = (acc * cs_ref[...]).astype(BF16)


def _proj_call(x2, mod4, wp, row_main, n, row_v, d_kv, colscale, wk, kg, kb, seq):
    m, d = x2.shape
    main_blk = row_main // TN_PROJ
    v_blk = row_v // d_kv
    tps = seq // TM
    small = lambda i, j: (i, 0)
    full = lambda i, j: (0, 0)
    return pl.pallas_call(
        _proj_kernel,
        out_shape=(jax.ShapeDtypeStruct((m, n), BF16),
                   jax.ShapeDtypeStruct((m, LANES), BF16),
                   jax.ShapeDtypeStruct((m, LANES), BF16),
                   jax.ShapeDtypeStruct((LANES, m), F32),
                   jax.ShapeDtypeStruct((m // TK, N_KV_HEADS * V_ROWS, TK), BF16)),
        grid=(m // TM, n // TN_PROJ),
        in_specs=[pl.BlockSpec((TM, d), small),
                  pl.BlockSpec((None, None, 1, d), lambda i, j: (i // tps, 1, 0, 0)),
                  pl.BlockSpec((None, None, 1, d), lambda i, j: (i // tps, 0, 0, 0)),
                  pl.BlockSpec((TN_PROJ, d), lambda i, j: (main_blk + j, 0)),
                  pl.BlockSpec((1, TN_PROJ), lambda i, j: (0, j)),
                  pl.BlockSpec((LANES, d), full),
                  pl.BlockSpec((1, LANES), full),
                  pl.BlockSpec((1, LANES), full),
                  pl.BlockSpec((d_kv, d), lambda i, j: (v_blk, 0))],
        out_specs=(pl.BlockSpec((TM, TN_PROJ), lambda i, j: (i, j)),
                   pl.BlockSpec((TM, LANES), small),
                   pl.BlockSpec((TM, LANES), small),
                   pl.BlockSpec((LANES, TM), lambda i, j: (0, i)),
                   pl.BlockSpec((TM // TK, N_KV_HEADS * V_ROWS, TK), lambda i, j: (i, 0, 0))),
        scratch_shapes=[pltpu.VMEM((TM, d), BF16)],
        compiler_params=_params(("arbitrary", "arbitrary")),
        name="proj",
    )(x2, mod4, mod4, wp, colscale, wk, kg, kb, wp)


RED_ROWS = 32
CNT_ROWS = 16
BITS_PER_CHECK = 4
SEARCH_GROUPS = -(-31 // BITS_PER_CHECK)
POS_BITS = 16


def _attn_kernel(q_ref, k_ref, vt_ref, qi_ref, ka_ref, kb_ref, wit_ref, o_ref,
                 score_ref, pos_ref, sa_ref, sb_ref, ma_ref, mb_ref, p_ref, m_ref, acc_ref):
    i = pl.program_id(1)
    q0 = i * TQ
    n_kt = (q0 + TQ) // TK
    q_pos = q0 + lax.broadcasted_iota(I32, (1, TQ), 1)
    limit = (q_pos // CHUNK + 1) * CHUNK
    key_iota = lax.broadcasted_iota(I32, (TK, TQ), 0)

    def score_body(j, carry):
        ks = pl.multiple_of(j * TK, TK)
        ka = ka_ref[pl.ds(ks, TK), :]
        kb = kb_ref[pl.ds(ks, TK), :]
        acc = None
        for p in range(IDX_HEADS // 2):
            qp = qi_ref[:, p * LANES:(p + 1) * LANES]
            d0 = lax.dot_general(ka, qp, _NT, preferred_element_type=F32)
            d1 = lax.dot_general(kb, qp, _NT, preferred_element_type=F32)
            w0 = wit_ref[IDX_DIM + 2 * p:IDX_DIM + 2 * p + 1, :]
            w1 = wit_ref[IDX_DIM + 2 * p + 1:IDX_DIM + 2 * p + 2, :]
            t = w0 * jnp.maximum(d0, 0.0) + w1 * jnp.maximum(d1, 0.0)
            acc = t if acc is None else acc + t
        score_ref[j] = jnp.where(ks + key_iota < limit, acc, -jnp.inf)
        return carry

    lax.fori_loop(0, n_kt, score_body, 0)

    def key_to_f32(key):
        return pltpu.bitcast(key ^ ((key >> 31) & 0x7FFFFFFF), F32)

    def count_ge(key):
        cand = key_to_f32(key)

        def body(j, c):
            for r in range(TK // CNT_ROWS):
                c = c + jnp.where(score_ref[j, r * CNT_ROWS:(r + 1) * CNT_ROWS, :] >= cand, 1.0, 0.0)
            return c

        c = lax.fori_loop(0, n_kt, body, jnp.zeros((CNT_ROWS, TQ), F32))
        return jnp.sum(c, axis=0, keepdims=True)

    kf = float(TOPK_MAX)
    c0 = count_ge(jnp.zeros((1, TQ), I32))
    t0 = jnp.where(c0 >= kf, 0, INT_MIN).astype(I32)
    ct0 = jnp.where(c0 >= kf, c0, (n_kt * TK).astype(F32))

    def search_cond(carry):
        g, _, ct = carry
        unsettled = jnp.where(ct == kf, 0.0, jnp.where(limit <= TOPK_MAX, 0.0, 1.0))
        return jnp.logical_and(g < SEARCH_GROUPS, jnp.max(unsettled) > 0.0)

    def search_body(carry):
        g, t, ct = carry
        for s in range(BITS_PER_CHECK):
            b = 30 - (g * BITS_PER_CHECK + s)
            bit = jnp.where(b >= 0, lax.shift_left(jnp.int32(1), jnp.maximum(b, 0)), 0)
            cand = t | bit
            c = count_ge(cand)
            keep = c >= kf
            t = jnp.where(keep, cand, t)
            ct = jnp.where(keep, c, ct)
        return g + 1, t, ct

    _, thr_key, cnt_thr = lax.while_loop(search_cond, search_body, (jnp.int32(0), t0, ct0))
    tie = jnp.where(cnt_thr > kf, jnp.where(thr_key > KEY_NEG_INF, 1.0, 0.0), 0.0)
    thr_key = jnp.maximum(thr_key, KEY_LOWEST_FINITE)
    thr = key_to_f32(thr_key)
    pos_ref[...] = jnp.full(pos_ref.shape, 2 ** POS_BITS, I32)

    @pl.when(jnp.max(tie) > 0.0)
    def _():
        keep_tied = kf - count_ge(thr_key + 1)

        def count_tied_below(pos):
            def body(j, c):
                idx = j * TK + key_iota
                hit = jnp.where(score_ref[j] == thr, jnp.where(idx < pos, 1.0, 0.0), 0.0)
                return c + jnp.sum(hit.reshape(TK // CNT_ROWS, CNT_ROWS, TQ), axis=0)

            c = lax.fori_loop(0, n_kt, body, jnp.zeros((CNT_ROWS, TQ), F32))
            return jnp.sum(c, axis=0, keepdims=True)

        def pos_body(b, pos):
            cand = pos | lax.shift_left(jnp.int32(1), POS_BITS - 1 - b)
            return jnp.where(count_tied_below(cand) <= keep_tied, cand, pos)

        pos = lax.fori_loop(0, POS_BITS, pos_body, jnp.zeros((1, TQ), I32))
        pos_ref[...] = jnp.where(tie > 0.0, pos, pos_ref[...])

    pos = pos_ref[...]

    def bias_body(j, carry):
        sc = score_ref[j]
        idx = j * TK + key_iota
        tied = jnp.where(sc == thr, jnp.where(idx < pos, 0.0, NEG_BIAS), NEG_BIAS)
        score_ref[j] = jnp.where(sc > thr, 0.0, tied)
        return carry

    lax.fori_loop(0, n_kt, bias_body, 0)

    n_ch = TK // RED_ROWS
    for n in range(N_KV_HEADS):
        qg = [q_ref[:, (n * GROUP + g) * HEAD_DIM:(n * GROUP + g + 1) * HEAD_DIM] for g in range(GROUP)]
        m_ref[...] = jnp.full(m_ref.shape, M_INIT, F32)
        acc_ref[...] = jnp.zeros(acc_ref.shape, F32)

        def logits(j, dst_ref, max_ref, n=n, qg=qg):
            ks = pl.multiple_of(j * TK, TK)
            kt = k_ref[pl.ds(ks, TK), n * HEAD_DIM:(n + 1) * HEAD_DIM]
            bias = score_ref[j]
            for g in range(GROUP):
                s = lax.dot_general(kt, qg[g], _NT, preferred_element_type=F32) + bias
                dst_ref[g] = s
                part = jnp.max(s.reshape(n_ch, RED_ROWS, TQ), axis=0)
                max_ref[g] = jnp.max(part, axis=0, keepdims=True)

        def softmax_pv(j, src_ref, max_ref, slot, n=n):
            vt = vt_ref[j, n * V_ROWS:(n + 1) * V_ROWS, :]
            for g in range(GROUP):
                m_old = m_ref[g]
                m_new = jnp.maximum(m_old, max_ref[g])
                alpha = jnp.exp2(m_old - m_new)
                m_ref[g] = m_new
                for c in range(n_ch):
                    rows = slice(c * RED_ROWS, (c + 1) * RED_ROWS)
                    p_ref[slot, g, rows, :] = jnp.exp2((src_ref[g, rows, :] - m_new).astype(BF16))
                acc_ref[g] = alpha * acc_ref[g] + jnp.dot(vt, p_ref[slot, g], preferred_element_type=F32)

        logits(0, sa_ref, ma_ref)

        def pair_body(jj, carry):
            j = 2 * jj
            logits(j + 1, sb_ref, mb_ref)
            softmax_pv(j, sa_ref, ma_ref, 0)
            logits(jnp.minimum(j + 2, n_kt - 1), sa_ref, ma_ref)
            softmax_pv(j + 1, sb_ref, mb_ref, 1)
            return carry

        lax.fori_loop(0, n_kt // 2, pair_body, 0)
        for g in range(GROUP):
            h = n * GROUP + g
            o = acc_ref[g, 0:HEAD_DIM, :] / acc_ref[g, HEAD_DIM:HEAD_DIM + 1, :]
            o_ref[:, h * HEAD_DIM:(h + 1) * HEAD_DIM] = o.T.astype(BF16)


def _attn_call(p2, ka, kb, wit, vt, batch, seq, cols):
    d_attn = N_HEADS * HEAD_DIM
    d_kv = N_KV_HEADS * HEAD_DIM
    d_qi = IDX_HEADS * IDX_DIM
    nq = seq // TQ
    nkt = seq // TK
    once = pl.Buffered(1)
    return pl.pallas_call(
        _attn_kernel,
        out_shape=jax.ShapeDtypeStruct((batch * seq, d_attn), BF16),
        grid=(batch, nq),
        in_specs=[pl.BlockSpec((TQ, d_attn), lambda b, i: (b * nq + i, cols["q"] // d_attn)),
                  pl.BlockSpec((seq, d_kv), lambda b, i: (b, cols["k"] // d_kv), pipeline_mode=once),
                  pl.BlockSpec((nkt, N_KV_HEADS * V_ROWS, TK), lambda b, i: (b, 0, 0), pipeline_mode=once),
                  pl.BlockSpec((TQ, d_qi), lambda b, i: (b * nq + i, cols["qi"] // d_qi)),
                  pl.BlockSpec((seq, LANES), lambda b, i: (b, 0), pipeline_mode=once),
                  pl.BlockSpec((seq, LANES), lambda b, i: (b, 0), pipeline_mode=once),
                  pl.BlockSpec((LANES, TQ), lambda b, i: (0, b * nq + i))],
        out_specs=pl.BlockSpec((TQ, d_attn), lambda b, i: (b * nq + i, 0)),
        scratch_shapes=[pltpu.VMEM((nkt, TK, TQ), F32),
                        pltpu.VMEM((1, TQ), I32),
                        pltpu.VMEM((GROUP, TK, TQ), F32),
                        pltpu.VMEM((GROUP, TK, TQ), F32),
                        pltpu.VMEM((GROUP, 1, TQ), F32),
                        pltpu.VMEM((GROUP, 1, TQ), F32),
                        pltpu.VMEM((2, GROUP, TK, TQ), BF16),
                        pltpu.VMEM((GROUP, 1, TQ), F32),
                        pltpu.VMEM((GROUP, V_ROWS, TQ), F32)],
        compiler_params=_params(("arbitrary", "arbitrary")),
        name="attn",
    )(p2, p2, vt, p2, ka, kb, wit)


def _mix_kernel(alpha, nj, ya_ref, yb_ref, wa_ref, wb_ref, ga_ref, gb_ref, wo_ref, x_ref, gate_ref,
                g_ref, b_ref, o_ref, acc_ref):
    j = pl.program_id(1)

    @pl.when(j == 0)
    def _():
        acc_ref[...] = jnp.zeros(acc_ref.shape, F32)

    ya = ya_ref[...]
    yb = yb_ref[...]
    halves = [slice(h * MIX_HALF, (h + 1) * MIX_HALF) for h in range(TN_MIX // MIX_HALF)]
    br = [(jnp.dot(ya, wa_ref[:, cols], preferred_element_type=F32),
           jnp.dot(yb, wb_ref[:, cols], preferred_element_type=F32)) for cols in halves]
    contrib = None
    for cols, (a, b) in zip(halves, br):
        merged = (_sigmoid(ga_ref[:, cols].astype(F32)) * a
                  + _sigmoid(gb_ref[:, cols].astype(F32)) * b).astype(BF16)
        part = jnp.dot(merged, wo_ref[cols, :], preferred_element_type=F32)
        contrib = part if contrib is None else contrib + part
    acc_ref[...] += contrib

    @pl.when(j == nj - 1)
    def _():
        r = alpha * x_ref[...] + (1.0 + gate_ref[...]) * acc_ref[...]
        o_ref[...] = _layer_norm_rows(r, g_ref[...], b_ref[...])


def _mix_call(ya, yb, wa, wb, p2, cols, wo, x2, mod4, g, b, seq, alpha):
    m, d = x2.shape
    dc = ya.shape[1]
    db = yb.shape[1]
    n = wa.shape[1]
    nj = n // TN_MIX
    ga_blk = cols["ga"] // TN_MIX
    gb_blk = cols["gb"] // TN_MIX
    tps = seq // TM_MIX
    row = lambda i, j: (i, 0)
    full = lambda i, j: (0, 0)
    return pl.pallas_call(
        functools.partial(_mix_kernel, alpha, nj),
        out_shape=jax.ShapeDtypeStruct((m, d), F32),
        grid=(m // TM_MIX, nj),
        in_specs=[pl.BlockSpec((TM_MIX, dc), row),
                  pl.BlockSpec((TM_MIX, db), row),
                  pl.BlockSpec((dc, TN_MIX), lambda i, j: (0, j)),
                  pl.BlockSpec((db, TN_MIX), lambda i, j: (0, j)),
                  pl.BlockSpec((TM_MIX, TN_MIX), lambda i, j: (i, ga_blk + j)),
                  pl.BlockSpec((TM_MIX, TN_MIX), lambda i, j: (i, gb_blk + j)),
                  pl.BlockSpec((TN_MIX, d), lambda i, j: (j, 0)),
                  pl.BlockSpec((TM_MIX, d), row),
                  pl.BlockSpec((None, None, 1, d), lambda i, j: (i // tps, 2, 0, 0)),
                  pl.BlockSpec((1, d), full),
                  pl.BlockSpec((1, d), full)],
        out_specs=pl.BlockSpec((TM_MIX, d), row),
        scratch_shapes=[pltpu.VMEM((TM_MIX, d), F32)],
        compiler_params=_params(("arbitrary", "arbitrary")),
        name="mix",
    )(ya, yb, wa, wb, p2, p2, wo, x2, mod4, g, b)


def _gelu_tanh(x):
    return 0.5 * x * (1.0 + jnp.tanh(0.7978845608028654 * (x + 0.044715 * (x * x * x))))


def _ffn_kernel(alpha, tiles_per_seq, nf, x_ref, sc_ref, sh_ref, gate_ref, wa_ref, wb_ref, cw_ref,
                wd_ref, g_ref, b_ref, o_ref, u_ref, acc_ref, halo_ref):
    i = pl.program_id(0)
    j = pl.program_id(1)

    @pl.when(j == 0)
    def _():
        u_ref[...] = (x_ref[...] * (1.0 + sc_ref[...]) + sh_ref[...]).astype(BF16)
        acc_ref[...] = jnp.zeros(acc_ref.shape, F32)

    @pl.when((i == 0) & (j == 0))
    def _():
        halo_ref[...] = jnp.zeros(halo_ref.shape, F32)

    u = u_ref[...]
    first = (i % tiles_per_seq) == 0
    halves = [slice(h * FFN_HALF, (h + 1) * FFN_HALF) for h in range(TF_FFN // FFN_HALF)]
    up = [(jnp.dot(u, wa_ref[:, cols], preferred_element_type=F32),
           jnp.dot(u, wb_ref[:, cols], preferred_element_type=F32)) for cols in halves]
    contrib = None
    for cols, (ha, hb) in zip(halves, up):
        conv = _causal_conv_rows(ha, first, halo_ref.at[:, :, cols], j, cw_ref.at[:, cols])
        act = (_gelu_tanh(conv) * hb).astype(BF16)
        part = jnp.dot(act, wd_ref[cols, :], preferred_element_type=F32)
        contrib = part if contrib is None else contrib + part
    acc_ref[...] += contrib

    @pl.when(j == nf - 1)
    def _():
        r = alpha * x_ref[...] + (1.0 + gate_ref[...]) * acc_ref[...]
        o_ref[...] = _layer_norm_rows(r, g_ref[...], b_ref[...])


def _ffn_call(x1, mod4, wup, cw, wdown, g, b, seq, alpha):
    m, d = x1.shape
    dff = wdown.shape[0]
    nf = dff // TF_FFN
    tps = seq // TM_FFN
    row = lambda i, j: (i, 0)
    full = lambda i, j: (0, 0)
    return pl.pallas_call(
        functools.partial(_ffn_kernel, alpha, tps, nf),
        out_shape=jax.ShapeDtypeStruct((m, d), F32),
        grid=(m // TM_FFN, nf),
        in_specs=[pl.BlockSpec((TM_FFN, d), row),
                  pl.BlockSpec((None, None, 1, d), lambda i, j: (i // tps, 4, 0, 0)),
                  pl.BlockSpec((None, None, 1, d), lambda i, j: (i // tps, 3, 0, 0)),
                  pl.BlockSpec((None, None, 1, d), lambda i, j: (i // tps, 5, 0, 0)),
                  pl.BlockSpec((d, TF_FFN), lambda i, j: (0, j)),
                  pl.BlockSpec((d, TF_FFN), lambda i, j: (0, nf + j)),
                  pl.BlockSpec((SUBLANES, TF_FFN), lambda i, j: (0, j)),
                  pl.BlockSpec((TF_FFN, d), lambda i, j: (j, 0)),
                  pl.BlockSpec((1, d), full),
                  pl.BlockSpec((1, d), full)],
        out_specs=pl.BlockSpec((TM_FFN, d), row),
        scratch_shapes=[pltpu.VMEM((TM_FFN, d), BF16),
                        pltpu.VMEM((TM_FFN, d), F32),
                        pltpu.VMEM((nf, SUBLANES, TF_FFN), F32)],
        compiler_params=_params(("arbitrary", "arbitrary")),
        name="ffn",
    )(x1, mod4, mod4, mod4, wup, wup, cw, wdown, g, b)


def _pad_rows(a, rows):
    return jnp.pad(a, ((0, rows - a.shape[0]), (0, 0)))


def kernel(x, c, w_cond, b_cond, w_in, conv_a, idx_kn_g, idx_kn_b, w_a, w_b, w_o, ln1_g, ln1_b,
           w_up, conv_f, w_down, ln2_g, ln2_b):
    batch, seq, d = x.shape
    depth = w_cond.shape[0]
    alpha = (2.0 * depth) ** 0.25
    d_attn = N_HEADS * HEAD_DIM
    d_kv = N_KV_HEADS * HEAD_DIM
    d_qi = IDX_HEADS * IDX_DIM
    d_conv = conv_a.shape[2]
    assert min(TOPK_MAX, seq // 4) == TOPK_MAX and seq % TM == 0 and TM % TK == 0 and TQ % (2 * TK) == 0
    assert seq < 2 ** (POS_BITS - 1)
    o_q = 3 * d_conv
    o_k = o_q + d_attn
    o_v = o_k + d_kv
    o_qi = o_v + d_kv
    o_ki = o_qi + d_qi
    o_g = o_ki + IDX_DIM + IDX_HEADS
    cols = {"q": 0, "qi": d_attn, "k": d_attn + d_qi, "ga": d_attn + d_qi + d_kv,
            "gb": d_attn + d_qi + d_kv + d}
    q_scale = HEAD_DIM ** -0.5 * math.log2(math.e)
    colscale = jnp.concatenate([jnp.full((1, d_attn), q_scale, F32),
                                jnp.ones((1, d_qi + d_kv + 2 * d), F32)], axis=1)

    x2 = x.reshape(batch * seq, d)
    c_pad = _pad_rows(c, 2 * SUBLANES)
    for l in range(depth):
        mod = _mod_call(c_pad, w_cond[l], b_cond[l][None, :])[:batch]
        mod4 = mod.reshape(batch, N_MOD, 1, d)

        wt = jnp.swapaxes(w_in[l], 0, 1)
        groups = ((0, o_q), (o_q, o_k), (o_qi, o_ki), (o_k, o_v), (o_g, o_g + 2 * d), (o_v, o_qi))
        src_rows = tuple(r for lo, hi in groups for r in range(lo, hi, REPACK_ROWS))
        wp = _repack_call(wt, src_rows)
        row_main, n_main, row_v = o_q, d_attn + d_qi + d_kv + 2 * d, o_q + d_attn + d_qi + d_kv + 2 * d
        wk = jnp.pad(wt[o_ki:o_g], ((0, LANES - (o_g - o_ki)), (0, 0))).astype(BF16)
        kg = jnp.pad(idx_kn_g[l], (0, LANES - IDX_DIM))[None, :]
        kb = jnp.pad(idx_kn_b[l], (0, LANES - IDX_DIM))[None, :]

        y_a, wa_b, wb_b, wo_b, wdown_b = _conv_a_call(x2, mod4, wp, d_conv, _pad_rows(conv_a[l], SUBLANES), seq,
                                                      (w_a[l], w_b[l], w_o[l], w_down[l]))
        p2, ka, kbo, wit, vt = _proj_call(x2, mod4, wp, row_main, n_main, row_v, d_kv, colscale, wk, kg, kb, seq)
        y_b = _attn_call(p2, ka, kbo, wit, vt, batch, seq, cols)
        x2 = _mix_call(y_a, y_b, wa_b, wb_b, p2, cols, wo_b,
                       x2, mod4, ln1_g[l][None, :], ln1_b[l][None, :], seq, alpha)
        x2 = _ffn_call(x2, mod4, w_up[l].astype(BF16), _pad_rows(conv_f[l], SUBLANES),
                       wdown_b, ln2_g[l][None, :], ln2_b[l][None, :], seq, alpha)
    return x2.reshape(batch, seq, d)
```

```python
import functools
import math

import jax
import jax.numpy as jnp
from jax import lax
from jax.experimental import pallas as pl
from jax.experimental.pallas import tpu as pltpu

F32 = jnp.float32
BF16 = jnp.bfloat16
I32 = jnp.int32

CHUNK = 64
N_HEADS = 16
N_KV_HEADS = 4
HEAD_DIM = 128
GROUP = N_HEADS // N_KV_HEADS
IDX_HEADS = 16
IDX_DIM = 64
IDX_W_SCALE = (IDX_HEADS * IDX_DIM) ** -0.5
TOPK_MAX = 256
LN_EPS = 1e-5
N_MOD = 6

LANES = 128
SUBLANES = 8
VMEM_LIMIT = 56 * 1024 * 1024

TM = 1024
TN_CONV = 512
CONV_HALF = 256
TN_PROJ = 1536
TM_MIX = 512
TN_MIX = 512
MIX_HALF = 256
REPACK_ROWS = 512
TM_FFN = 512
TF_FFN = 512
FFN_HALF = 256
TQ = 512
TK = 256
ONES_ROWS = 16
V_ROWS = HEAD_DIM + ONES_ROWS

NEG_BIAS = -1e30
M_INIT = -3e38
INT_MIN = -(2 ** 31)
KEY_NEG_INF = INT_MIN + 0x7FFFFF
KEY_LOWEST_FINITE = KEY_NEG_INF + 1

_NT = (((1,), (1,)), ((), ()))


def _params(sem):
    return pltpu.CompilerParams(dimension_semantics=sem, vmem_limit_bytes=VMEM_LIMIT)


def _sigmoid(x):
    return 1.0 / (1.0 + jnp.exp(-x))


def _layer_norm_rows(r, g, b):
    mu = jnp.mean(r, axis=-1, keepdims=True)
    d = r - mu
    var = jnp.mean(d * d, axis=-1, keepdims=True)
    return d * lax.rsqrt(var + LN_EPS) * g + b


def _repack_kernel(src_ref, w_ref, o_ref):
    del src_ref
    o_ref[...] = w_ref[...].astype(BF16)


def _repack_call(wt, src_rows):
    d = wt.shape[1]
    nb = len(src_rows)
    return pl.pallas_call(
        _repack_kernel,
        out_shape=jax.ShapeDtypeStruct((nb * REPACK_ROWS, d), BF16),
        grid_spec=pltpu.PrefetchScalarGridSpec(
            num_scalar_prefetch=1,
            grid=(nb,),
            in_specs=[pl.BlockSpec((pl.Element(REPACK_ROWS), pl.Element(d)),
                                   lambda b, src: (pl.multiple_of(src[b], SUBLANES), 0))],
            out_specs=pl.BlockSpec((REPACK_ROWS, d), lambda b, src: (b, 0))),
        compiler_params=_params(("arbitrary",)),
        name="repack",
    )(jnp.asarray(src_rows, I32), wt)


def _mod_kernel(c_ref, w_ref, b_ref, o_ref):
    c = c_ref[...]
    ca = (c * _sigmoid(c)).astype(BF16)
    o_ref[...] = jnp.dot(ca, w_ref[...].astype(BF16), preferred_element_type=F32) + b_ref[...]


def _mod_call(c_pad, w_cond, b_cond):
    rows, d = c_pad.shape
    n = w_cond.shape[1]
    tn = 1024
    return pl.pallas_call(
        _mod_kernel,
        out_shape=jax.ShapeDtypeStruct((rows, n), F32),
        grid=(n // tn,),
        in_specs=[pl.BlockSpec((rows, d), lambda j: (0, 0)),
                  pl.BlockSpec((d, tn), lambda j: (0, j)),
                  pl.BlockSpec((1, tn), lambda j: (0, j))],
        out_specs=pl.BlockSpec((rows, tn), lambda j: (0, j)),
        compiler_params=_params(("arbitrary",)),
        name="mod",
    )(c_pad, w_cond, b_cond)


def _causal_conv_rows(p, first, halo_ref, j, cw_ref):
    tm = p.shape[0]
    halo = jnp.where(first, 0.0, halo_ref[j])
    halo_ref[j] = p[tm - SUBLANES:, :]
    ext = jnp.concatenate([halo, p], axis=0)
    prev1 = pltpu.roll(ext, 1, axis=0)[SUBLANES:, :]
    prev2 = pltpu.roll(ext, 2, axis=0)[SUBLANES:, :]
    return cw_ref[0:1, :] * prev2 + cw_ref[1:2, :] * prev1 + cw_ref[2:3, :] * p


def _conv_a_kernel(tiles_per_seq, n_cast, x_ref, sc_ref, sh_ref, wb_ref, wc_ref, wh_ref, cw_ref, *refs):
    cast_in, (o_ref, *cast_out), (u_ref, halo_ref) = refs[:n_cast], refs[n_cast:2 * n_cast + 1], refs[2 * n_cast + 1:]
    i = pl.program_id(0)
    j = pl.program_id(1)
    for src_ref, dst_ref in zip(cast_in, cast_out):
        dst_ref[...] = src_ref[...].astype(BF16)

    @pl.when(j == 0)
    def _():
        u_ref[...] = (x_ref[...] * (1.0 + sc_ref[...]) + sh_ref[...]).astype(BF16)

    @pl.when((i == 0) & (j == 0))
    def _():
        halo_ref[...] = jnp.zeros(halo_ref.shape, F32)

    u = u_ref[...]
    first = (i % tiles_per_seq) == 0
    halves = [slice(h * CONV_HALF, (h + 1) * CONV_HALF) for h in range(TN_CONV // CONV_HALF)]
    proj = [tuple(lax.dot_general(u, w_ref[cols, :], _NT, preferred_element_type=F32)
                  for w_ref in (wb_ref, wc_ref, wh_ref)) for cols in halves]
    for cols, (cb, cc, ch) in zip(halves, proj):
        conv = _causal_conv_rows(cc * ch, first, halo_ref.at[:, :, cols], j, cw_ref.at[:, cols])
        o_ref[:, cols] = (cb * conv).astype(BF16)


def _conv_a_call(x2, mod4, wp, dc, cw, seq, cast_ws):
    m, d = x2.shape
    nj = dc // TN_CONV
    tps = seq // TM
    steps = (m // TM) * nj
    row = lambda i, j: (i, 0)
    slab = lambda i, j: (i * nj + j, 0)
    cast_blocks = [(w.shape[0] // steps, w.shape[1]) for w in cast_ws]
    assert all(w.shape[0] % (steps * 2 * SUBLANES) == 0 for w in cast_ws)
    return pl.pallas_call(
        functools.partial(_conv_a_kernel, tps, len(cast_ws)),
        out_shape=(jax.ShapeDtypeStruct((m, dc), BF16),
                   *[jax.ShapeDtypeStruct(w.shape, BF16) for w in cast_ws]),
        grid=(m // TM, nj),
        in_specs=[pl.BlockSpec((TM, d), row),
                  pl.BlockSpec((None, None, 1, d), lambda i, j: (i // tps, 1, 0, 0)),
                  pl.BlockSpec((None, None, 1, d), lambda i, j: (i // tps, 0, 0, 0)),
                  pl.BlockSpec((TN_CONV, d), lambda i, j: (j, 0)),
                  pl.BlockSpec((TN_CONV, d), lambda i, j: (nj + j, 0)),
                  pl.BlockSpec((TN_CONV, d), lambda i, j: (2 * nj + j, 0)),
                  pl.BlockSpec((SUBLANES, TN_CONV), lambda i, j: (0, j)),
                  *[pl.BlockSpec(b, slab) for b in cast_blocks]],
        out_specs=(pl.BlockSpec((TM, TN_CONV), lambda i, j: (i, j)),
                   *[pl.BlockSpec(b, slab) for b in cast_blocks]),
        scratch_shapes=[pltpu.VMEM((TM, d), BF16),
                        pltpu.VMEM((nj, SUBLANES, TN_CONV), F32)],
        compiler_params=_params(("arbitrary", "arbitrary")),
        name="conv_a",
    )(x2, mod4, mod4, wp, wp, wp, cw, *cast_ws)


def _proj_kernel(x_ref, sc_ref, sh_ref, w_ref, cs_ref, wk_ref, kg_ref, kb_ref, wvt_ref,
                 o_ref, ka_ref, kbo_ref, wit_ref, vt_ref, u_ref):
    j = pl.program_id(1)

    @pl.when(j == 0)
    def _():
        ub = (x_ref[...] * (1.0 + sc_ref[...]) + sh_ref[...]).astype(BF16)
        u_ref[...] = ub
        kw = lax.dot_general(ub, wk_ref[...], _NT, preferred_element_type=F32)
        lane = lax.broadcasted_iota(I32, (1, LANES), 1)
        is_key = lane < IDX_DIM
        mu = jnp.sum(jnp.where(is_key, kw, 0.0), axis=-1, keepdims=True) * (1.0 / IDX_DIM)
        dk = jnp.where(is_key, kw - mu, 0.0)
        var = jnp.sum(dk * dk, axis=-1, keepdims=True) * (1.0 / IDX_DIM)
        kn = dk * lax.rsqrt(var + LN_EPS) * kg_ref[...] + kb_ref[...]
        ka_ref[...] = kn.astype(BF16)
        kbo_ref[...] = pltpu.roll(kn, IDX_DIM, axis=1).astype(BF16)
        wit_ref[...] = (kw * IDX_W_SCALE).T
        vt = lax.dot_general(wvt_ref[...], ub, _NT, preferred_element_type=F32).astype(BF16)
        ones = jnp.ones((ONES_ROWS, TK), BF16)
        for t in range(vt_ref.shape[0]):
            for n in range(N_KV_HEADS):
                vt_ref[t, n * V_ROWS:n * V_ROWS + HEAD_DIM, :] = (
                    vt[n * HEAD_DIM:(n + 1) * HEAD_DIM, t * TK:(t + 1) * TK])
                vt_ref[t, n * V_ROWS + HEAD_DIM:(n + 1) * V_ROWS, :] = ones

    acc = lax.dot_general(u_ref[...], w_ref[...], _NT, preferred_element_type=F32)
    o_ref[...] = (acc * cs_ref[...]).astype(BF16)


def _proj_call(x2, mod4, wp, row_main, n, row_v, d_kv, colscale, wk, kg, kb, seq):
    m, d = x2.shape
    main_blk = row_main // TN_PROJ
    v_blk = row_v // d_kv
    tps = seq // TM
    small = lambda i, j: (i, 0)
    full = lambda i, j: (0, 0)
    return pl.pallas_call(
        _proj_kernel,
        out_shape=(jax.ShapeDtypeStruct((m, n), BF16),
                   jax.ShapeDtypeStruct((m, LANES), BF16),
                   jax.ShapeDtypeStruct((m, LANES), BF16),
                   jax.ShapeDtypeStruct((LANES, m), F32),
                   jax.ShapeDtypeStruct((m // TK, N_KV_HEADS * V_ROWS, TK), BF16)),
        grid=(m // TM, n // TN_PROJ),
        in_specs=[pl.BlockSpec((TM, d), small),
                  pl.BlockSpec((None, None, 1, d), lambda i, j: (i // tps, 1, 0, 0)),
                  pl.BlockSpec((None, None, 1, d), lambda i, j: (i // tps, 0, 0, 0)),
                  pl.BlockSpec((TN_PROJ, d), lambda i, j: (main_blk + j, 0)),
                  pl.BlockSpec((1, TN_PROJ), lambda i, j: (0, j)),
                  pl.BlockSpec((LANES, d), full),
                  pl.BlockSpec((1, LANES), full),
                  pl.BlockSpec((1, LANES), full),
                  pl.BlockSpec((d_kv, d), lambda i, j: (v_blk, 0))],
        out_specs=(pl.BlockSpec((TM, TN_PROJ), lambda i, j: (i, j)),
                   pl.BlockSpec((TM, LANES), small),
                   pl.BlockSpec((TM, LANES), small),
                   pl.BlockSpec((LANES, TM), lambda i, j: (0, i)),
                   pl.BlockSpec((TM // TK, N_KV_HEADS * V_ROWS, TK), lambda i, j: (i, 0, 0))),
        scratch_shapes=[pltpu.VMEM((TM, d), BF16)],
        compiler_params=_params(("arbitrary", "arbitrary")),
        name="proj",
    )(x2, mod4, mod4, wp, colscale, wk, kg, kb, wp)


RED_ROWS = 32
CNT_ROWS = 16
BITS_PER_CHECK = 4
SEARCH_GROUPS = -(-31 // BITS_PER_CHECK)
POS_BITS = 16


def _attn_kernel(q_ref, k_ref, vt_ref, qi_ref, ka_ref, kb_ref, wit_ref, o_ref,
                 score_ref, pos_ref, sa_ref, sb_ref, ma_ref, mb_ref, p_ref, m_ref, acc_ref):
    i = pl.program_id(1)
    q0 = i * TQ
    n_kt = (q0 + TQ) // TK
    q_pos = q0 + lax.broadcasted_iota(I32, (1, TQ), 1)
    limit = (q_pos // CHUNK + 1) * CHUNK
    key_iota = lax.broadcasted_iota(I32, (TK, TQ), 0)

    def score_body(j, carry):
        ks = pl.multiple_of(j * TK, TK)
        ka = ka_ref[pl.ds(ks, TK), :]
        kb = kb_ref[pl.ds(ks, TK), :]
        acc = None
        for p in range(IDX_HEADS // 2):
            qp = qi_ref[:, p * LANES:(p + 1) * LANES]
            d0 = lax.dot_general(ka, qp, _NT, preferred_element_type=F32)
            d1 = lax.dot_general(kb, qp, _NT, preferred_element_type=F32)
            w0 = wit_ref[IDX_DIM + 2 * p:IDX_DIM + 2 * p + 1, :]
            w1 = wit_ref[IDX_DIM + 2 * p + 1:IDX_DIM + 2 * p + 2, :]
            t = w0 * jnp.maximum(d0, 0.0) + w1 * jnp.maximum(d1, 0.0)
            acc = t if acc is None else acc + t
        score_ref[j] = jnp.where(ks + key_iota < limit, acc, -jnp.inf)
        return carry

    lax.fori_loop(0, n_kt, score_body, 0)

    def key_to_f32(key):
        return pltpu.bitcast(key ^ ((key >> 31) & 0x7FFFFFFF), F32)

    def count_ge(key):
        cand = key_to_f32(key)

        def body(j, c):
            for r in range(TK // CNT_ROWS):
                c = c + jnp.where(score_ref[j, r * CNT_ROWS:(r + 1) * CNT_ROWS, :] >= cand, 1.0, 0.0)
            return c

        c = lax.fori_loop(0, n_kt, body, jnp.zeros((CNT_ROWS, TQ), F32))
        return jnp.sum(c, axis=0, keepdims=True)

    kf = float(TOPK_MAX)
    c0 = count_ge(jnp.zeros((1, TQ), I32))
    t0 = jnp.where(c0 >= kf, 0, INT_MIN).astype(I32)
    ct0 = jnp.where(c0 >= kf, c0, (n_kt * TK).astype(F32))

    def search_cond(carry):
        g, _, ct = carry
        unsettled = jnp.where(ct == kf, 0.0, jnp.where(limit <= TOPK_MAX, 0.0, 1.0))
        return jnp.logical_and(g < SEARCH_GROUPS, jnp.max(unsettled) > 0.0)

    def search_body(carry):
        g, t, ct = carry
        for s in range(BITS_PER_CHECK):
            b = 30 - (g * BITS_PER_CHECK + s)
            bit = jnp.where(b >= 0, lax.shift_left(jnp.int32(1), jnp.maximum(b, 0)), 0)
            cand = t | bit
            c = count_ge(cand)
            keep = c >= kf
            t = jnp.where(keep, cand, t)
            ct = jnp.where(keep, c, ct)
        return g + 1, t, ct

    _, thr_key, cnt_thr = lax.while_loop(search_cond, search_body, (jnp.int32(0), t0, ct0))
    tie = jnp.where(cnt_thr > kf, jnp.where(thr_key > KEY_NEG_INF, 1.0, 0.0), 0.0)
    thr_key = jnp.maximum(thr_key, KEY_LOWEST_FINITE)
    thr = key_to_f32(thr_key)
    pos_ref[...] = jnp.full(pos_ref.shape, 2 ** POS_BITS, I32)

    @pl.when(jnp.max(tie) > 0.0)
    def _():
        keep_tied = kf - count_ge(thr_key + 1)

        def count_tied_below(pos):
            def body(j, c):
                idx = j * TK + key_iota
                hit = jnp.where(score_ref[j] == thr, jnp.where(idx < pos, 1.0, 0.0), 0.0)
                return c + jnp.sum(hit.reshape(TK // CNT_ROWS, CNT_ROWS, TQ), axis=0)

            c = lax.fori_loop(0, n_kt, body, jnp.zeros((CNT_ROWS, TQ), F32))
            return jnp.sum(c, axis=0, keepdims=True)

        def pos_body(b, pos):
            cand = pos | lax.shift_left(jnp.int32(1), POS_BITS - 1 - b)
            return jnp.where(count_tied_below(cand) <= keep_tied, cand, pos)

        pos = lax.fori_loop(0, POS_BITS, pos_body, jnp.zeros((1, TQ), I32))
        pos_ref[...] = jnp.where(tie > 0.0, pos, pos_ref[...])

    pos = pos_ref[...]

    def bias_body(j, carry):
        sc = score_ref[j]
        idx = j * TK + key_iota
        tied = jnp.where(sc == thr, jnp.where(idx < pos, 0.0, NEG_BIAS), NEG_BIAS)
        score_ref[j] = jnp.where(sc > thr, 0.0, tied)
        return carry

    lax.fori_loop(0, n_kt, bias_body, 0)

    n_ch = TK // RED_ROWS
    for n in range(N_KV_HEADS):
        qg = [q_ref[:, (n * GROUP + g) * HEAD_DIM:(n * GROUP + g + 1) * HEAD_DIM] for g in range(GROUP)]
        m_ref[...] = jnp.full(m_ref.shape, M_INIT, F32)
        acc_ref[...] = jnp.zeros(acc_ref.shape, F32)

        def logits(j, dst_ref, max_ref, n=n, qg=qg):
            ks = pl.multiple_of(j * TK, TK)
            kt = k_ref[pl.ds(ks, TK), n * HEAD_DIM:(n + 1) * HEAD_DIM]
            bias = score_ref[j]
            for g in range(GROUP):
                s = lax.dot_general(kt, qg[g], _NT, preferred_element_type=F32) + bias
                dst_ref[g] = s
                part = jnp.max(s.reshape(n_ch, RED_ROWS, TQ), axis=0)
                max_ref[g] = jnp.max(part, axis=0, keepdims=True)

        def softmax_pv(j, src_ref, max_ref, slot, n=n):
            vt = vt_ref[j, n * V_ROWS:(n + 1) * V_ROWS, :]
            for g in range(GROUP):
                m_old = m_ref[g]
                m_new = jnp.maximum(m_old, max_ref[g])
                alpha = jnp.exp2(m_old - m_new)
                m_ref[g] = m_new
                for c in range(n_ch):
                    rows = slice(c * RED_ROWS, (c + 1) * RED_ROWS)
                    p_ref[slot, g, rows, :] = jnp.exp2((src_ref[g, rows, :] - m_new).astype(BF16))
                acc_ref[g] = alpha * acc_ref[g] + jnp.dot(vt, p_ref[slot, g], preferred_element_type=F32)

        logits(0, sa_ref, ma_ref)

        def pair_body(jj, carry):
            j = 2 * jj
            logits(j + 1, sb_ref, mb_ref)
            softmax_pv(j, sa_ref, ma_ref, 0)
            logits(jnp.minimum(j + 2, n_kt - 1), sa_ref, ma_ref)
            softmax_pv(j + 1, sb_ref, mb_ref, 1)
            return carry

        lax.fori_loop(0, n_kt // 2, pair_body, 0)
        for g in range(GROUP):
            h = n * GROUP + g
            o = acc_ref[g, 0:HEAD_DIM, :] / acc_ref[g, HEAD_DIM:HEAD_DIM + 1, :]
            o_ref[:, h * HEAD_DIM:(h + 1) * HEAD_DIM] = o.T.astype(BF16)


def _attn_call(p2, ka, kb, wit, vt, batch, seq, cols):
    d_attn = N_HEADS * HEAD_DIM
    d_kv = N_KV_HEADS * HEAD_DIM
    d_qi = IDX_HEADS * IDX_DIM
    nq = seq // TQ
    nkt = seq // TK
    once = pl.Buffered(1)
    return pl.pallas_call(
        _attn_kernel,
        out_shape=jax.ShapeDtypeStruct((batch * seq, d_attn), BF16),
        grid=(batch, nq),
        in_specs=[pl.BlockSpec((TQ, d_attn), lambda b, i: (b * nq + i, cols["q"] // d_attn)),
                  pl.BlockSpec((seq, d_kv), lambda b, i: (b, cols["k"] // d_kv), pipeline_mode=once),
                  pl.BlockSpec((nkt, N_KV_HEADS * V_ROWS, TK), lambda b, i: (b, 0, 0), pipeline_mode=once),
                  pl.BlockSpec((TQ, d_qi), lambda b, i: (b * nq + i, cols["qi"] // d_qi)),
                  pl.BlockSpec((seq, LANES), lambda b, i: (b, 0), pipeline_mode=once),
                  pl.BlockSpec((seq, LANES), lambda b, i: (b, 0), pipeline_mode=once),
                  pl.BlockSpec((LANES, TQ), lambda b, i: (0, b * nq + i))],
        out_specs=pl.BlockSpec((TQ, d_attn), lambda b, i: (b * nq + i, 0)),
        scratch_shapes=[pltpu.VMEM((nkt, TK, TQ), F32),
                        pltpu.VMEM((1, TQ), I32),
                        pltpu.VMEM((GROUP, TK, TQ), F32),
                        pltpu.VMEM((GROUP, TK, TQ), F32),
                        pltpu.VMEM((GROUP, 1, TQ), F32),
                        pltpu.VMEM((GROUP, 1, TQ), F32),
                        pltpu.VMEM((2, GROUP, TK, TQ), BF16),
                        pltpu.VMEM((GROUP, 1, TQ), F32),
                        pltpu.VMEM((GROUP, V_ROWS, TQ), F32)],
        compiler_params=_params(("arbitrary", "arbitrary")),
        name="attn",
    )(p2, p2, vt, p2, ka, kb, wit)


def _mix_kernel(alpha, nj, ya_ref, yb_ref, wa_ref, wb_ref, ga_ref, gb_ref, wo_ref, x_ref, gate_ref,
                g_ref, b_ref, cast_ref, o_ref, cast_out_ref, acc_ref):
    j = pl.program_id(1)
    cast_out_ref[...] = cast_ref[...].astype(BF16)

    @pl.when(j == 0)
    def _():
        acc_ref[...] = jnp.zeros(acc_ref.shape, F32)

    ya = ya_ref[...]
    yb = yb_ref[...]
    halves = [slice(h * MIX_HALF, (h + 1) * MIX_HALF) for h in range(TN_MIX // MIX_HALF)]
    br = [(jnp.dot(ya, wa_ref[:, cols], preferred_element_type=F32),
           jnp.dot(yb, wb_ref[:, cols], preferred_element_type=F32)) for cols in halves]
    contrib = None
    for cols, (a, b) in zip(halves, br):
        merged = (_sigmoid(ga_ref[:, cols].astype(F32)) * a
                  + _sigmoid(gb_ref[:, cols].astype(F32)) * b).astype(BF16)
        part = jnp.dot(merged, wo_ref[cols, :], preferred_element_type=F32)
        contrib = part if contrib is None else contrib + part
    acc_ref[...] += contrib

    @pl.when(j == nj - 1)
    def _():
        r = alpha * x_ref[...] + (1.0 + gate_ref[...]) * acc_ref[...]
        o_ref[...] = _layer_norm_rows(r, g_ref[...], b_ref[...])


def _mix_call(ya, yb, wa, wb, p2, cols, wo, x2, mod4, g, b, seq, alpha, cast_w):
    m, d = x2.shape
    dc = ya.shape[1]
    db = yb.shape[1]
    n = wa.shape[1]
    nj = n // TN_MIX
    ga_blk = cols["ga"] // TN_MIX
    gb_blk = cols["gb"] // TN_MIX
    tps = seq // TM_MIX
    row = lambda i, j: (i, 0)
    full = lambda i, j: (0, 0)
    steps = (m // TM_MIX) * nj
    assert cast_w.shape[0] % (steps * 2 * SUBLANES) == 0
    cast_spec = pl.BlockSpec((cast_w.shape[0] // steps, cast_w.shape[1]), lambda i, j: (i * nj + j, 0))
    return pl.pallas_call(
        functools.partial(_mix_kernel, alpha, nj),
        out_shape=(jax.ShapeDtypeStruct((m, d), F32), jax.ShapeDtypeStruct(cast_w.shape, BF16)),
        grid=(m // TM_MIX, nj),
        in_specs=[pl.BlockSpec((TM_MIX, dc), row),
                  pl.BlockSpec((TM_MIX, db), row),
                  pl.BlockSpec((dc, TN_MIX), lambda i, j: (0, j)),
                  pl.BlockSpec((db, TN_MIX), lambda i, j: (0, j)),
                  pl.BlockSpec((TM_MIX, TN_MIX), lambda i, j: (i, ga_blk + j)),
                  pl.BlockSpec((TM_MIX, TN_MIX), lambda i, j: (i, gb_blk + j)),
                  pl.BlockSpec((TN_MIX, d), lambda i, j: (j, 0)),
                  pl.BlockSpec((TM_MIX, d), row),
                  pl.BlockSpec((None, None, 1, d), lambda i, j: (i // tps, 2, 0, 0)),
                  pl.BlockSpec((1, d), full),
                  pl.BlockSpec((1, d), full),
                  cast_spec],
        out_specs=(pl.BlockSpec((TM_MIX, d), row), cast_spec),
        scratch_shapes=[pltpu.VMEM((TM_MIX, d), F32)],
        compiler_params=_params(("arbitrary", "arbitrary")),
        name="mix",
    )(ya, yb, wa, wb, p2, p2, wo, x2, mod4, g, b, cast_w)


def _gelu_tanh(x):
    return 0.5 * x * (1.0 + jnp.tanh(0.7978845608028654 * (x + 0.044715 * (x * x * x))))


def _ffn_kernel(alpha, tiles_per_seq, nf, x_ref, sc_ref, sh_ref, gate_ref, wa_ref, wb_ref, cw_ref,
                wd_ref, g_ref, b_ref, o_ref, u_ref, acc_ref, halo_ref):
    i = pl.program_id(0)
    j = pl.program_id(1)

    @pl.when(j == 0)
    def _():
        u_ref[...] = (x_ref[...] * (1.0 + sc_ref[...]) + sh_ref[...]).astype(BF16)
        acc_ref[...] = jnp.zeros(acc_ref.shape, F32)

    @pl.when((i == 0) & (j == 0))
    def _():
        halo_ref[...] = jnp.zeros(halo_ref.shape, F32)

    u = u_ref[...]
    first = (i % tiles_per_seq) == 0
    halves = [slice(h * FFN_HALF, (h + 1) * FFN_HALF) for h in range(TF_FFN // FFN_HALF)]
    up = [(jnp.dot(u, wa_ref[:, cols], preferred_element_type=F32),
           jnp.dot(u, wb_ref[:, cols], preferred_element_type=F32)) for cols in halves]
    contrib = None
    for cols, (ha, hb) in zip(halves, up):
        conv = _causal_conv_rows(ha, first, halo_ref.at[:, :, cols], j, cw_ref.at[:, cols])
        act = (_gelu_tanh(conv) * hb).astype(BF16)
        part = jnp.dot(act, wd_ref[cols, :], preferred_element_type=F32)
        contrib = part if contrib is None else contrib + part
    acc_ref[...] += contrib

    @pl.when(j == nf - 1)
    def _():
        r = alpha * x_ref[...] + (1.0 + gate_ref[...]) * acc_ref[...]
        o_ref[...] = _layer_norm_rows(r, g_ref[...], b_ref[...])


def _ffn_call(x1, mod4, wup, cw, wdown, g, b, seq, alpha):
    m, d = x1.shape
    dff = wdown.shape[0]
    nf = dff // TF_FFN
    tps = seq // TM_FFN
    row = lambda i, j: (i, 0)
    full = lambda i, j: (0, 0)
    return pl.pallas_call(
        functools.partial(_ffn_kernel, alpha, tps, nf),
        out_shape=jax.ShapeDtypeStruct((m, d), F32),
        grid=(m // TM_FFN, nf),
        in_specs=[pl.BlockSpec((TM_FFN, d), row),
                  pl.BlockSpec((None, None, 1, d), lambda i, j: (i // tps, 4, 0, 0)),
                  pl.BlockSpec((None, None, 1, d), lambda i, j: (i // tps, 3, 0, 0)),
                  pl.BlockSpec((None, None, 1, d), lambda i, j: (i // tps, 5, 0, 0)),
                  pl.BlockSpec((d, TF_FFN), lambda i, j: (0, j)),
                  pl.BlockSpec((d, TF_FFN), lambda i, j: (0, nf + j)),
                  pl.BlockSpec((SUBLANES, TF_FFN), lambda i, j: (0, j)),
                  pl.BlockSpec((TF_FFN, d), lambda i, j: (j, 0)),
                  pl.BlockSpec((1, d), full),
                  pl.BlockSpec((1, d), full)],
        out_specs=pl.BlockSpec((TM_FFN, d), row),
        scratch_shapes=[pltpu.VMEM((TM_FFN, d), BF16),
                        pltpu.VMEM((TM_FFN, d), F32),
                        pltpu.VMEM((nf, SUBLANES, TF_FFN), F32)],
        compiler_params=_params(("arbitrary", "arbitrary")),
        name="ffn",
    )(x1, mod4, mod4, mod4, wup, wup, cw, wdown, g, b)


def _pad_rows(a, rows):
    return jnp.pad(a, ((0, rows - a.shape[0]), (0, 0)))


def kernel(x, c, w_cond, b_cond, w_in, conv_a, idx_kn_g, idx_kn_b, w_a, w_b, w_o, ln1_g, ln1_b,
           w_up, conv_f, w_down, ln2_g, ln2_b):
    batch, seq, d = x.shape
    depth = w_cond.shape[0]
    alpha = (2.0 * depth) ** 0.25
    d_attn = N_HEADS * HEAD_DIM
    d_kv = N_KV_HEADS * HEAD_DIM
    d_qi = IDX_HEADS * IDX_DIM
    d_conv = conv_a.shape[2]
    assert min(TOPK_MAX, seq // 4) == TOPK_MAX and seq % TM == 0 and TM % TK == 0 and TQ % (2 * TK) == 0
    assert seq < 2 ** (POS_BITS - 1)
    o_q = 3 * d_conv
    o_k = o_q + d_attn
    o_v = o_k + d_kv
    o_qi = o_v + d_kv
    o_ki = o_qi + d_qi
    o_g = o_ki + IDX_DIM + IDX_HEADS
    cols = {"q": 0, "qi": d_attn, "k": d_attn + d_qi, "ga": d_attn + d_qi + d_kv,
            "gb": d_attn + d_qi + d_kv + d}
    q_scale = HEAD_DIM ** -0.5 * math.log2(math.e)
    colscale = jnp.concatenate([jnp.full((1, d_attn), q_scale, F32),
                                jnp.ones((1, d_qi + d_kv + 2 * d), F32)], axis=1)

    x2 = x.reshape(batch * seq, d)
    c_pad = _pad_rows(c, 2 * SUBLANES)
    for l in range(depth):
        mod = _mod_call(c_pad, w_cond[l], b_cond[l][None, :])[:batch]
        mod4 = mod.reshape(batch, N_MOD, 1, d)

        wt = jnp.swapaxes(w_in[l], 0, 1)
        groups = ((0, o_q), (o_q, o_k), (o_qi, o_ki), (o_k, o_v), (o_g, o_g + 2 * d), (o_v, o_qi))
        src_rows = tuple(r for lo, hi in groups for r in range(lo, hi, REPACK_ROWS))
        wp = _repack_call(wt, src_rows)
        row_main, n_main, row_v = o_q, d_attn + d_qi + d_kv + 2 * d, o_q + d_attn + d_qi + d_kv + 2 * d
        wk = jnp.pad(wt[o_ki:o_g], ((0, LANES - (o_g - o_ki)), (0, 0))).astype(BF16)
        kg = jnp.pad(idx_kn_g[l], (0, LANES - IDX_DIM))[None, :]
        kb = jnp.pad(idx_kn_b[l], (0, LANES - IDX_DIM))[None, :]

        y_a, wa_b, wb_b, wo_b, wdown_b = _conv_a_call(x2, mod4, wp, d_conv, _pad_rows(conv_a[l], SUBLANES), seq,
                                                      (w_a[l], w_b[l], w_o[l], w_down[l]))
        p2, ka, kbo, wit, vt = _proj_call(x2, mod4, wp, row_main, n_main, row_v, d_kv, colscale, wk, kg, kb, seq)
        y_b = _attn_call(p2, ka, kbo, wit, vt, batch, seq, cols)
        x2, wup_b = _mix_call(y_a, y_b, wa_b, wb_b, p2, cols, wo_b,
                              x2, mod4, ln1_g[l][None, :], ln1_b[l][None, :], seq, alpha, w_up[l])
        x2 = _ffn_call(x2, mod4, wup_b, _pad_rows(conv_f[l], SUBLANES),
                       wdown_b, ln2_g[l][None, :], ln2_b[l][None, :], seq, alpha)
    return x2.reshape(batch, seq, d)
```
